```python
import math
import jax
import jax.numpy as jnp
from jax import lax
import numpy as np

D_MODEL = 1024
BATCH = 4
SEQ = 4096
DEPTH = 1
DEC_BATCH = 32
DEC_SEQ = 1
PAST_LEN = 16384
PAGE_SIZE = 128

H_DIFF = 4
DH_DIFF = 64
DV_DIFF = 2 * DH_DIFF
H_MOBA = 8
DH_MOBA = 64
MOBA_BLOCK = 256
MOBA_TOPK = 3
MOBA_Q_BLOCK = 32
N_MEM = 256
H_MEM = 4
DH_MEM = 128
D_FF = 4 * D_MODEL
Q_BLOCK = 128
EPS = 1e-6
NEG_INF = -1e30

D_DIFF_QK = H_DIFF * 2 * DH_DIFF
D_DIFF_V = H_DIFF * DV_DIFF
D_MOBA = H_MOBA * DH_MOBA
D_MIX = D_DIFF_V + D_MOBA
D_IN = 2 * D_DIFF_QK + D_DIFF_V + 3 * D_MOBA
D_MEM_ATTN = H_MEM * DH_MEM

kernel_name = 'hymba_style_diffattn_moba_decoder_step'


def rmsnorm(x, g):
    xf = x.astype(jnp.float32)
    y = xf * lax.rsqrt(jnp.mean(xf * xf, axis=-1, keepdims=True) + EPS)
    return (y * g.astype(jnp.float32)).astype(x.dtype)


def alibi_slopes(n):
    return jnp.asarray(np.power(2.0, -8.0 * np.arange(1, n + 1) / n).astype(np.float32))


def mixer_inputs(x, g_pre, w_in):
    B, T = x.shape[:2]
    z = jnp.einsum('btd,de->bte', rmsnorm(x, g_pre), w_in)
    s1 = D_DIFF_QK
    s2 = s1 + D_DIFF_QK
    s3 = s2 + D_DIFF_V
    s4 = s3 + D_MOBA
    s5 = s4 + D_MOBA
    dq, dk, dv, mq, mk, mv = jnp.split(z, [s1, s2, s3, s4, s5], axis=-1)
    return (dq.reshape(B, T, H_DIFF, 2, DH_DIFF), dk.reshape(B, T, H_DIFF, 2, DH_DIFF),
            dv.reshape(B, T, H_DIFF, DV_DIFF), mq.reshape(B, T, H_MOBA, DH_MOBA),
            mk.reshape(B, T, H_MOBA, DH_MOBA), mv.reshape(B, T, H_MOBA, DH_MOBA))


def diff_attn(q, q_pos, segs, lam, lam_init, g_subln):
    slopes = alibi_slopes(H_DIFF)
    qf = q.astype(jnp.float32) * (DH_DIFF ** -0.5)
    parts = []
    for k, v, k_pos in segs:
        s = jnp.einsum('bqhcd,bkhcd->bhcqk', qf, k.astype(jnp.float32))
        dist = (q_pos[:, None] - k_pos[None, :]).astype(jnp.float32)
        bias = -slopes[:, None, None] * dist[None]
        parts.append(jnp.where((dist >= 0)[None, None, None], s + bias[None, :, None], NEG_INF))
    p = jax.nn.softmax(jnp.concatenate(parts, axis=-1), axis=-1)
    w = p[:, :, 0] - lam * p[:, :, 1]
    out = None
    off = 0
    for k, v, k_pos in segs:
        n = k.shape[1]
        o = jnp.einsum('bhqk,bkhe->bqhe', w[..., off:off + n], v.astype(jnp.float32))
        out = o if out is None else out + o
        off += n
    out = rmsnorm(out, g_subln) * (1.0 - lam_init)
    return out.astype(q.dtype)


def diff_prompt(q, k, v, lam, lam_init, g_subln):
    B, T = q.shape[:2]
    nq = T // Q_BLOCK
    pos = jnp.arange(T, dtype=jnp.int32)
    qb = q.reshape(B, nq, Q_BLOCK, H_DIFF, 2, DH_DIFF).swapaxes(0, 1)

    def one(args):
        qi, i = args
        qp = i * Q_BLOCK + jnp.arange(Q_BLOCK, dtype=jnp.int32)
        return diff_attn(qi, qp, ((k, v, pos),), lam, lam_init, g_subln)

    out = lax.map(one, (qb, jnp.arange(nq, dtype=jnp.int32)))
    return out.swapaxes(0, 1).reshape(B, T, H_DIFF, DV_DIFF)


def moba_blocks(k, v):
    B, L, H, D = k.shape
    nb = -(-L // MOBA_BLOCK)
    pad = nb * MOBA_BLOCK - L
    kb = jnp.pad(k, ((0, 0), (0, pad), (0, 0), (0, 0))).reshape(B, nb, MOBA_BLOCK, H, D)
    vb = jnp.pad(v, ((0, 0), (0, pad), (0, 0), (0, 0))).reshape(B, nb, MOBA_BLOCK, H, D)
    kmean = jnp.mean(kb.astype(jnp.float32), axis=2)
    return kb, vb, kmean


def moba_attn(q, q_pos, kb, vb, kmean):
    B, Tq, H, D = q.shape
    nb = kb.shape[1]
    slopes = alibi_slopes(H_MOBA)
    qf = q.astype(jnp.float32) * (D ** -0.5)
    gate = jnp.einsum('bqhd,bnhd->bhqn', qf, kmean)
    own = q_pos // MOBA_BLOCK
    fully_past = jnp.arange(nb)[None, :] < own[:, None]
    gate = jnp.where(fully_past[None, None], gate, NEG_INF)
    _, sel = lax.top_k(gate, min(MOBA_TOPK, nb))
    own_b = jnp.broadcast_to(own[None, None, :, None], (B, H, Tq, 1)).astype(sel.dtype)
    blk = jnp.concatenate([sel, own_b], axis=-1)
    valid = jnp.concatenate([sel < own_b, jnp.ones(own_b.shape, dtype=bool)], axis=-1)
    b_idx = jnp.arange(B)[:, None, None, None]
    h_idx = jnp.arange(H)[None, :, None, None]
    kg = kb[b_idx, blk, :, h_idx].astype(jnp.float32)
    vg = vb[b_idx, blk, :, h_idx].astype(jnp.float32)
    s = jnp.einsum('bqhd,bhqskd->bhqsk', qf, kg)
    k_pos = blk[..., None] * MOBA_BLOCK + jnp.arange(MOBA_BLOCK)
    dist = (q_pos[None, None, :, None, None] - k_pos).astype(jnp.float32)
    mask = valid[..., None] & (dist >= 0)
    s = jnp.where(mask, s - slopes[None, :, None, None, None] * dist, NEG_INF)
    ns = blk.shape[-1]
    p = jax.nn.softmax(s.reshape(B, H, Tq, ns * MOBA_BLOCK), axis=-1).reshape(s.shape)
    out = jnp.einsum('bhqsk,bhqskd->bqhd', p, vg)
    return out.astype(q.dtype)


def moba_prompt(q, k, v):
    B, T, H, D = q.shape
    kb, vb, kmean = moba_blocks(k, v)
    nq = T // MOBA_Q_BLOCK
    qb = q.reshape(B, nq, MOBA_Q_BLOCK, H, D).swapaxes(0, 1)

    def one(args):
        qi, i = args
        qp = i * MOBA_Q_BLOCK + jnp.arange(MOBA_Q_BLOCK, dtype=jnp.int32)
        return moba_attn(qi, qp, kb, vb, kmean)

    out = lax.map(one, (qb, jnp.arange(nq, dtype=jnp.int32)))
    return out.swapaxes(0, 1).reshape(B, T, H, D)


def mem_kv(mem, g_mem, w_mk, w_mv):
    B, N = mem.shape[:2]
    h = rmsnorm(mem, g_mem)
    k = jnp.einsum('bnd,de->bne', h, w_mk).reshape(B, N, H_MEM, DH_MEM)
    v = jnp.einsum('bnd,de->bne', h, w_mv).reshape(B, N, H_MEM, DH_MEM)
    return k, v


def mem_attn(q, mk, mv):
    s = jnp.einsum('bqhd,bkhd->bhqk', q.astype(jnp.float32) * (DH_MEM ** -0.5), mk.astype(jnp.float32))
    p = jax.nn.softmax(s, axis=-1)
    return jnp.einsum('bhqk,bkhd->bqhd', p, mv.astype(jnp.float32)).astype(q.dtype)


def finish_layer(x, o_diff, o_moba, mem_k, mem_v, w_out, g_post_mix, g_pre_x, w_xq, w_xo,
                 g_post_x, g_pre_ff, w_up, w_down, g_post_ff):
    B, T = x.shape[:2]
    o = jnp.concatenate([o_diff.reshape(B, T, D_DIFF_V), o_moba.reshape(B, T, D_MOBA)], axis=-1)
    x = x + rmsnorm(jnp.einsum('bte,ed->btd', o, w_out), g_post_mix)
    q = jnp.einsum('btd,de->bte', rmsnorm(x, g_pre_x), w_xq).reshape(B, T, H_MEM, DH_MEM)
    c = mem_attn(q, mem_k, mem_v).reshape(B, T, D_MEM_ATTN)
    x = x + rmsnorm(jnp.einsum('bte,ed->btd', c, w_xo), g_post_x)
    hdn = jnp.square(jax.nn.relu(jnp.einsum('btd,df->btf', rmsnorm(x, g_pre_ff), w_up)))
    x = x + rmsnorm(jnp.einsum('btf,fd->btd', hdn, w_down), g_post_ff)
    return x


def setup_inputs(seed: int = 0) -> dict:
    key = jax.random.key(seed)
    ks = jax.random.split(key, 40)
    f = jnp.float32
    n_pages = PAST_LEN // PAGE_SIZE
    n_used = DEC_BATCH * n_pages
    n_pool = n_used + max(1, n_used // 4)

    def nrm(k, shape, scale=1.0):
        return jax.random.normal(k, shape, f) * scale

    def gain(k, n):
        return 1.0 + 0.05 * jax.random.normal(k, (DEPTH, n), f)

    page_table = jax.random.permutation(ks[0], n_pool)[:n_used].reshape(DEC_BATCH, n_pages).astype(jnp.int32)
    return {
        'x_prompt': nrm(ks[1], (BATCH, SEQ, D_MODEL)),
        'x_sample': nrm(ks[2], (DEC_BATCH, DEC_SEQ, D_MODEL)),
        'cache_diff_k': nrm(ks[3], (DEPTH, n_pool, PAGE_SIZE, H_DIFF, 2, DH_DIFF)),
        'cache_diff_v': nrm(ks[4], (DEPTH, n_pool, PAGE_SIZE, H_DIFF, DV_DIFF)),
        'cache_moba_k': nrm(ks[5], (DEPTH, n_pool, PAGE_SIZE, H_MOBA, DH_MOBA)),
        'cache_moba_v': nrm(ks[6], (DEPTH, n_pool, PAGE_SIZE, H_MOBA, DH_MOBA)),
        'cache_mem_k': nrm(ks[7], (DEPTH, DEC_BATCH, N_MEM, H_MEM, DH_MEM)),
        'cache_mem_v': nrm(ks[8], (DEPTH, DEC_BATCH, N_MEM, H_MEM, DH_MEM)),
        'page_table': page_table,
        'mem_prompt': nrm(ks[9], (BATCH, N_MEM, D_MODEL)),
        'g_pre_mix': gain(ks[10], D_MODEL),
        'w_in': nrm(ks[11], (DEPTH, D_MODEL, D_IN), D_MODEL ** -0.5),
        'lambda_q1': nrm(ks[12], (DEPTH, DH_DIFF), 0.1),
        'lambda_k1': nrm(ks[13], (DEPTH, DH_DIFF), 0.1),
        'lambda_q2': nrm(ks[14], (DEPTH, DH_DIFF), 0.1),
        'lambda_k2': nrm(ks[15], (DEPTH, DH_DIFF), 0.1),
        'g_subln': gain(ks[16], DV_DIFF),
        'w_out': nrm(ks[17], (DEPTH, D_MIX, D_MODEL), D_MIX ** -0.5),
        'g_post_mix': gain(ks[18], D_MODEL),
        'g_mem': gain(ks[19], D_MODEL),
        'w_mem_k': nrm(ks[20], (DEPTH, D_MODEL, D_MEM_ATTN), D_MODEL ** -0.5),
        'w_mem_v': nrm(ks[21], (DEPTH, D_MODEL, D_MEM_ATTN), D_MODEL ** -0.5),
        'g_pre_x': gain(ks[22], D_MODEL),
        'w_xq': nrm(ks[23], (DEPTH, D_MODEL, D_MEM_ATTN), D_MODEL ** -0.5),
        'w_xo': nrm(ks[24], (DEPTH, D_MEM_ATTN, D_MODEL), D_MEM_ATTN ** -0.5),
        'g_post_x': gain(ks[25], D_MODEL),
        'g_pre_ff': gain(ks[26], D_MODEL),
        'w_up': nrm(ks[27], (DEPTH, D_MODEL, D_FF), D_MODEL ** -0.5),
        'w_down': nrm(ks[28], (DEPTH, D_FF, D_MODEL), D_FF ** -0.5),
        'g_post_ff': gain(ks[29], D_MODEL),
    }


def reference(x_prompt, x_sample, cache_diff_k, cache_diff_v, cache_moba_k, cache_moba_v,
              cache_mem_k, cache_mem_v, page_table, mem_prompt,
              g_pre_mix, w_in, lambda_q1, lambda_k1, lambda_q2, lambda_k2, g_subln, w_out, g_post_mix,
              g_mem, w_mem_k, w_mem_v, g_pre_x, w_xq, w_xo, g_post_x,
              g_pre_ff, w_up, w_down, g_post_ff):
    xp = x_prompt
    xs = x_sample
    n_db, S = xs.shape[:2]
    past = page_table.shape[1] * PAGE_SIZE
    pos_past = jnp.arange(past, dtype=jnp.int32)
    pos_s = past + jnp.arange(S, dtype=jnp.int32)
    p_dk, p_dv, p_mk, p_mv, p_memk, p_memv = [], [], [], [], [], []
    s_dk, s_dv, s_mk, s_mv = [], [], [], []
    for l in range(DEPTH):
        lam_init = 0.8 - 0.6 * math.exp(-0.3 * l)
        lam = (jnp.exp(jnp.sum(lambda_q1[l].astype(jnp.float32) * lambda_k1[l].astype(jnp.float32)))
               - jnp.exp(jnp.sum(lambda_q2[l].astype(jnp.float32) * lambda_k2[l].astype(jnp.float32)))
               + lam_init)
        dq, dk, dv, mq, mk, mv = mixer_inputs(xp, g_pre_mix[l], w_in[l])
        o_d = diff_prompt(dq, dk, dv, lam, lam_init, g_subln[l])
        o_m = moba_prompt(mq, mk, mv)
        memk, memv = mem_kv(mem_prompt, g_mem[l], w_mem_k[l], w_mem_v[l])
        xp = finish_layer(xp, o_d, o_m, memk, memv, w_out[l], g_post_mix[l], g_pre_x[l], w_xq[l],
                          w_xo[l], g_post_x[l], g_pre_ff[l], w_up[l], w_down[l], g_post_ff[l])
        p_dk.append(dk)
        p_dv.append(dv)
        p_mk.append(mk)
        p_mv.append(mv)
        p_memk.append(memk)
        p_memv.append(memv)
        sq, sk, sv, tq, tk, tv = mixer_inputs(xs, g_pre_mix[l], w_in[l])
        dk_past = cache_diff_k[l][page_table].reshape(n_db, past, H_DIFF, 2, DH_DIFF)
        dv_past = cache_diff_v[l][page_table].reshape(n_db, past, H_DIFF, DV_DIFF)
        o_ds = diff_attn(sq, pos_s, ((dk_past, dv_past, pos_past), (sk, sv, pos_s)),
                         lam, lam_init, g_subln[l])
        mk_all = jnp.concatenate([cache_moba_k[l][page_table].reshape(n_db, past, H_MOBA, DH_MOBA), tk], axis=1)
        mv_all = jnp.concatenate([cache_moba_v[l][page_table].reshape(n_db, past, H_MOBA, DH_MOBA), tv], axis=1)
        kb, vb, kmean = moba_blocks(mk_all, mv_all)
        o_ms = moba_attn(tq, pos_s, kb, vb, kmean)
        xs = finish_layer(xs, o_ds, o_ms, cache_mem_k[l], cache_mem_v[l], w_out[l], g_post_mix[l],
                          g_pre_x[l], w_xq[l], w_xo[l], g_post_x[l], g_pre_ff[l], w_up[l],
                          w_down[l], g_post_ff[l])
        s_dk.append(sk)
        s_dv.append(sv)
        s_mk.append(tk)
        s_mv.append(tv)
    return (xp, xs,
            jnp.stack(p_dk), jnp.stack(p_dv), jnp.stack(p_mk), jnp.stack(p_mv),
            jnp.stack(p_memk), jnp.stack(p_memv),
            jnp.stack(s_dk), jnp.stack(s_dv), jnp.stack(s_mk), jnp.stack(s_mv))
```

```python
import functools
import math

import numpy as np
import jax
import jax.numpy as jnp
from jax import lax
from jax.experimental import pallas as pl
from jax.experimental.pallas import tpu as pltpu

F32 = jnp.float32
BF16 = jnp.bfloat16

EPS = 1e-6
NEG_INF = -1e30
REMOVED = -3e38

H_DIFF = 4
DH_DIFF = 64
DV_DIFF = 2 * DH_DIFF
H_MOBA = 8
DH_MOBA = 64
MOBA_BLOCK = 256
MOBA_TOPK = 3
H_MEM = 4
DH_MEM = 128
N_MEM = 256
GROUP = 512
LANES = 128
SEL_STRIDE = LANES // H_MOBA

VMEM_LIMIT_BYTES = 56 * 1024 * 1024


def _alibi_slopes(n):
    return np.power(2.0, -8.0 * np.arange(1, n + 1) / n).astype(np.float32)


def _params(sem):
    return pltpu.CompilerParams(dimension_semantics=sem, vmem_limit_bytes=VMEM_LIMIT_BYTES)


def _rms(x, g):
    return x * lax.rsqrt(jnp.mean(x * x, axis=-1, keepdims=True) + EPS) * g


def _dot(a, b):
    return jnp.dot(a, b, preferred_element_type=F32)


def _dot_nt(a, b, precision=None):
    return lax.dot_general(a, b, (((1,), (1,)), ((), ())), preferred_element_type=F32,
                           precision=precision)


def _proj_kernel(x_ref, g_ref, w_ref, dq_ref, dk_ref, dkb_ref, dv_ref, dvb_ref,
                 mq_ref, mk_ref, mkb_ref, mv_ref, mvb_ref, *rest, tm, n_blk_seq, with_select):
    xn = _rms(x_ref[...], g_ref[...]).astype(BF16)

    def col(c):
        return _dot(xn, w_ref[:, c * GROUP:(c + 1) * GROUP])

    dq_ref[...] = (col(0) * (DH_DIFF ** -0.5)).astype(BF16)
    z = col(1)
    dk_ref[...] = z
    dkb_ref[...] = z.astype(BF16)
    z = col(2)
    dv_ref[...] = z
    dvb_ref[...] = z.astype(BF16)
    mq = col(3) * (DH_MOBA ** -0.5)
    mq_ref[...] = mq.astype(BF16)
    mk = col(4)
    mk_ref[...] = mk
    mkb_ref[...] = mk.astype(BF16)
    z = col(5)
    mv_ref[...] = z
    mvb_ref[...] = z.astype(BF16)

    if not with_select:
        rest[0][...] = mq
        return
    sel_ref, kmt_ref = rest
    t = pl.program_id(0)
    blk_per_tile = tm // MOBA_BLOCK
    tiles_per_seq = n_blk_seq // blk_per_tile
    blk0 = (t % tiles_per_seq) * blk_per_tile

    @pl.when(t == 0)
    def _():
        kmt_ref[...] = jnp.zeros_like(kmt_ref)

    lane_g = lax.broadcasted_iota(jnp.int32, (1, GROUP), 1) // DH_MOBA
    for r in range(blk_per_tile):
        km = jnp.mean(mk[r * MOBA_BLOCK:(r + 1) * MOBA_BLOCK], axis=0, keepdims=True)
        for h in range(H_MOBA):
            kmt_ref[pl.ds(h * SEL_STRIDE + blk0 + r, 1), :] = jnp.where(lane_g == h, km, 0.0)

    gate = _dot_nt(mq, kmt_ref[...], precision=lax.Precision.HIGHEST)
    lane = lax.broadcasted_iota(jnp.int32, (tm, LANES), 1)
    row = lax.broadcasted_iota(jnp.int32, (tm, LANES), 0)
    n_in = lane % SEL_STRIDE
    own = blk0 + row // MOBA_BLOCK
    past = n_in < own
    g = jnp.where(past, gate, NEG_INF)
    cnt = jnp.zeros((tm, LANES), jnp.int32)
    for s in range(1, n_blk_seq):
        lo = pltpu.roll(g, s, 1)
        cnt += jnp.where((n_in >= s) & (lo >= g), 1, 0)
        hi = pltpu.roll(g, LANES - s, 1)
        cnt += jnp.where((n_in + s < SEL_STRIDE) & (hi > g), 1, 0)
    chosen = past & (cnt < MOBA_TOPK)
    sel_ref[...] = jnp.where(chosen, 0.0, NEG_INF).astype(BF16)


def _project(x2d, g, w_bf, *, tm, n_blk_seq=None):
    n, d = x2d.shape
    with_select = n_blk_seq is not None
    grid = (n // tm,)
    row_spec = pl.BlockSpec((tm, GROUP), lambda t: (t, 0))
    f32_out = jax.ShapeDtypeStruct((n, GROUP), F32)
    bf_out = jax.ShapeDtypeStruct((n, GROUP), BF16)
    out_shape = [bf_out, f32_out, bf_out, f32_out, bf_out, bf_out, f32_out, bf_out, f32_out, bf_out]
    out_specs = [row_spec] * 10
    scratch = []
    if with_select:
        assert tm % MOBA_BLOCK == 0 and n_blk_seq % (tm // MOBA_BLOCK) == 0
        assert n_blk_seq <= SEL_STRIDE
        out_shape.append(jax.ShapeDtypeStruct((n, LANES), BF16))
        out_specs.append(pl.BlockSpec((tm, LANES), lambda t: (t, 0)))
        scratch.append(pltpu.VMEM((LANES, GROUP), F32))
    else:
        out_shape.append(f32_out)
        out_specs.append(row_spec)
    return pl.pallas_call(
        functools.partial(_proj_kernel, tm=tm, n_blk_seq=n_blk_seq, with_select=with_select),
        grid=grid,
        in_specs=[pl.BlockSpec((tm, d), lambda t: (t, 0)),
                  pl.BlockSpec((1, d), lambda t: (0, 0)),
                  pl.BlockSpec((d, 6 * GROUP), lambda t: (0, 0))],
        out_specs=out_specs,
        out_shape=out_shape,
        scratch_shapes=scratch,
        compiler_params=_params(("arbitrary",)),
    )(x2d, g.reshape(1, d), w_bf)


def _pair_attn_kernel(slope_ref, q_ref, k_ref, v_ref, *rest, tq, moba, lam_init):
    if moba:
        sel_ref, o_ref, m_scr, l_scr, acc_scr = rest
    else:
        lamp_ref, g_ref, o_ref, m_scr, l_scr, acc_scr = rest
    p = pl.program_id(1)
    i = pl.program_id(2)
    half_w = LANES // 2

    q = q_ref[...]
    lane = lax.broadcasted_iota(jnp.int32, (tq, LANES), 1)
    zero = jnp.zeros_like(q)
    q2 = jnp.concatenate([jnp.where(lane < half_w, q, zero),
                          jnp.where(lane >= half_w, q, zero)], axis=0)
    if moba:
        slopes = (slope_ref[2 * p], slope_ref[2 * p + 1])
        sel = sel_ref[...]
        grp = lane // SEL_STRIDE
        sel2 = jnp.concatenate([jnp.where(grp == 2 * p, sel, jnp.zeros_like(sel)),
                                jnp.where(grp == 2 * p + 1, sel, jnp.zeros_like(sel))], axis=0)
        q2aug = jnp.concatenate([q2, sel2], axis=1)
    else:
        slopes = (slope_ref[p], slope_ref[p])

    m_scr[...] = jnp.full_like(m_scr, NEG_INF)
    l_scr[...] = jnp.zeros_like(l_scr)
    acc_scr[...] = jnp.zeros_like(acc_scr)

    iota_k = lax.broadcasted_iota(jnp.int32, (1, tq), 1)
    iota_q = lax.broadcasted_iota(jnp.int32, (tq, tq), 0)
    iota_kk = lax.broadcasted_iota(jnp.int32, (tq, tq), 1)
    causal = iota_q >= iota_kk
    key_lane = lax.broadcasted_iota(jnp.int32, (tq, LANES), 1)

    def block(jb, diag):
        start = pl.multiple_of(jb * tq, tq)
        kj = k_ref[pl.ds(start, tq), :]
        vj = v_ref[pl.ds(start, tq), :]
        if moba and not diag:
            onehot = jnp.where(key_lane % SEL_STRIDE == jb, 1.0, 0.0).astype(BF16)
            s2 = _dot_nt(q2aug, jnp.concatenate([kj, onehot], axis=1))
        else:
            s2 = _dot_nt(q2, kj)
        rel = (iota_k + (jb - i) * tq).astype(F32)
        for hf in range(2):
            s = s2[hf * tq:(hf + 1) * tq] + slopes[hf] * rel
            if diag:
                s = jnp.where(causal, s, NEG_INF)
            m_old = m_scr[hf]
            m_new = jnp.maximum(m_old, jnp.max(s, axis=-1, keepdims=True))
            alpha = jnp.exp(m_old - m_new)
            pe = jnp.exp(s - m_new)
            l_scr[hf] = alpha * l_scr[hf] + jnp.sum(pe, axis=-1, keepdims=True)
            acc_scr[hf] = alpha * acc_scr[hf] + _dot(pe.astype(BF16), vj)
            m_scr[hf] = m_new

    block(i, True)

    def body(jb, carry):
        block(jb, False)
        return carry

    lax.fori_loop(0, i, body, 0)

    o0 = acc_scr[0] / l_scr[0]
    o1 = acc_scr[1] / l_scr[1]
    if moba:
        o_ref[...] = jnp.where(lane < half_w, o0, o1).astype(o_ref.dtype)
    else:
        lp = lamp_ref[...]
        lam = (jnp.exp(jnp.sum(lp[0:1] * lp[1:2], axis=-1, keepdims=True))
               - jnp.exp(jnp.sum(lp[2:3] * lp[3:4], axis=-1, keepdims=True)) + lam_init)
        o = o0 - lam * o1
        o_ref[...] = (_rms(o, g_ref[...]) * (1.0 - lam_init)).astype(o_ref.dtype)


def _pair_attention(q_bf, k_bf, v_bf, slopes, *, nb, seq, tq, moba, sel=None, lamp=None,
                    g_subln=None, lam_init=0.0):
    n_pair = GROUP // LANES
    nq = seq // tq
    q_spec = pl.BlockSpec((tq, LANES), lambda b, p, i: (b * nq + i, p))
    kv_spec = pl.BlockSpec((seq, LANES), lambda b, p, i: (b, p))
    in_specs = [pl.BlockSpec(memory_space=pltpu.SMEM), q_spec, kv_spec, kv_spec]
    args = [jnp.asarray(slopes), q_bf, k_bf, v_bf]
    if moba:
        assert tq == MOBA_BLOCK
        in_specs.append(pl.BlockSpec((tq, LANES), lambda b, p, i: (b * nq + i, 0)))
        args.append(sel)
    else:
        in_specs += [pl.BlockSpec((4, DH_DIFF), lambda b, p, i: (0, 0)),
                     pl.BlockSpec((1, DV_DIFF), lambda b, p, i: (0, 0))]
        args += [lamp, g_subln.reshape(1, DV_DIFF)]
    return pl.pallas_call(
        functools.partial(_pair_attn_kernel, tq=tq, moba=moba, lam_init=lam_init),
        grid=(nb, n_pair, nq),
        in_specs=in_specs,
        out_specs=q_spec,
        out_shape=jax.ShapeDtypeStruct((nb * seq, GROUP), BF16),
        scratch_shapes=[pltpu.VMEM((2, tq, 1), F32), pltpu.VMEM((2, tq, 1), F32),
                        pltpu.VMEM((2, tq, LANES), F32)],
        compiler_params=_params(("arbitrary", "arbitrary", "arbitrary")),
    )(*args)


def _norm_mm_kernel(x_ref, g_ref, w_ref, o_ref, *, scale, relu2):
    y = _dot(_rms(x_ref[...], g_ref[...]).astype(BF16), w_ref[...])
    if relu2:
        y = jnp.square(jnp.maximum(y, 0.0))
    if scale != 1.0:
        y = y * scale
    o_ref[...] = y.astype(o_ref.dtype)


def _norm_matmul(x2d, g, w_bf, *, tm, out_dtype, scale=1.0, relu2=False):
    n, d = x2d.shape
    e = w_bf.shape[1]
    return pl.pallas_call(
        functools.partial(_norm_mm_kernel, scale=scale, relu2=relu2),
        grid=(n // tm,),
        in_specs=[pl.BlockSpec((tm, d), lambda t: (t, 0)),
                  pl.BlockSpec((1, d), lambda t: (0, 0)),
                  pl.BlockSpec((d, e), lambda t: (0, 0))],
        out_specs=pl.BlockSpec((tm, e), lambda t: (t, 0)),
        out_shape=jax.ShapeDtypeStruct((n, e), out_dtype),
        compiler_params=_params(("arbitrary",)),
    )(x2d, g.reshape(1, d), w_bf)


def _mm_norm_res_kernel(*refs, n_in):
    a_refs = refs[:n_in]
    w_refs = refs[n_in:2 * n_in]
    x_ref, g_ref, o_ref = refs[2 * n_in:]
    y = _dot(a_refs[0][...].astype(BF16), w_refs[0][...])
    for a_ref, w_ref in zip(a_refs[1:], w_refs[1:]):
        y = y + _dot(a_ref[...].astype(BF16), w_ref[...])
    o_ref[...] = x_ref[...] + _rms(y, g_ref[...])


def _matmul_norm_residual(a_list, w_list, x2d, g, *, tm):
    n, d = x2d.shape
    n_in = len(a_list)
    in_specs = [pl.BlockSpec((tm, a.shape[1]), lambda t: (t, 0)) for a in a_list]
    in_specs += [pl.BlockSpec(w.shape, lambda t: (0, 0)) for w in w_list]
    in_specs += [pl.BlockSpec((tm, d), lambda t: (t, 0)), pl.BlockSpec((1, d), lambda t: (0, 0))]
    return pl.pallas_call(
        functools.partial(_mm_norm_res_kernel, n_in=n_in),
        grid=(n // tm,),
        in_specs=in_specs,
        out_specs=pl.BlockSpec((tm, d), lambda t: (t, 0)),
        out_shape=jax.ShapeDtypeStruct((n, d), F32),
        compiler_params=_params(("arbitrary",)),
    )(*a_list, *w_list, x2d, g.reshape(1, d))


def _mem_attn_kernel(q_ref, k_ref, v_ref, o_ref):
    q = q_ref[...]
    outs = []
    for h in range(H_MEM):
        cols = slice(h * DH_MEM, (h + 1) * DH_MEM)
        s = _dot_nt(q[:, cols], k_ref[:, cols].astype(BF16))
        pe = jnp.exp(s - jnp.max(s, axis=-1, keepdims=True))
        o = _dot(pe.astype(BF16), v_ref[:, cols].astype(BF16))
        outs.append(o / jnp.sum(pe, axis=-1, keepdims=True))
    o_ref[...] = jnp.concatenate(outs, axis=1).astype(o_ref.dtype)


def _mem_attention(q_bf, mem_k, mem_v, *, nb, rows, tm):
    d = H_MEM * DH_MEM
    nt = rows // tm
    return pl.pallas_call(
        _mem_attn_kernel,
        grid=(nb, nt),
        in_specs=[pl.BlockSpec((tm, d), lambda b, t: (b * nt + t, 0)),
                  pl.BlockSpec((None, N_MEM, d), lambda b, t: (b, 0, 0)),
                  pl.BlockSpec((None, N_MEM, d), lambda b, t: (b, 0, 0))],
        out_specs=pl.BlockSpec((tm, d), lambda b, t: (b * nt + t, 0)),
        out_shape=jax.ShapeDtypeStruct((nb * rows, d), BF16),
        compiler_params=_params(("arbitrary", "arbitrary")),
    )(q_bf, mem_k, mem_v)


def _finish_layer(x2d, o_diff, o_moba, mem_k, mem_v, wts, *, nb, rows, tm, rows_pad=1):
    d = x2d.shape[1]
    x1 = _matmul_norm_residual([o_diff, o_moba], [wts['w_out_a'], wts['w_out_b']], x2d,
                               wts['g_post_mix'], tm=tm)
    q = _norm_matmul(x1, wts['g_pre_x'], wts['w_xq'], tm=tm, out_dtype=BF16, scale=DH_MEM ** -0.5)
    if rows_pad > 1:
        qp = jnp.broadcast_to(q[:, None, :], (nb * rows, rows_pad, q.shape[1])).reshape(-1, q.shape[1])
        c = _mem_attention(qp, mem_k, mem_v, nb=nb, rows=rows * rows_pad, tm=rows * rows_pad)
        c = c.reshape(nb * rows, rows_pad, -1)[:, 0, :]
    else:
        c = _mem_attention(q, mem_k, mem_v, nb=nb, rows=rows, tm=min(tm, rows))
    x2 = _matmul_norm_residual([c], [wts['w_xo']], x1, wts['g_post_x'], tm=tm)
    hdn = _norm_matmul(x2, wts['g_pre_ff'], wts['w_up'], tm=tm, out_dtype=BF16, relu2=True)
    return _matmul_norm_residual([hdn], [wts['w_down']], x2, wts['g_post_ff'], tm=tm)


def _diff_decode_kernel(pt_ref, q_ref, ks_ref, vs_ref, lamp_ref, g_ref, *rest, n_pg, past, lam_init):
    k_refs = rest[:n_pg]
    v_refs = rest[n_pg:2 * n_pg]
    o_ref, m_scr, l_scr, acc_scr = rest[2 * n_pg:]
    step = pl.program_id(1)
    n_rows = 2 * H_DIFF
    page = k_refs[0].shape[0]

    row = lax.broadcasted_iota(jnp.int32, (n_rows, GROUP), 0)
    lane = lax.broadcasted_iota(jnp.int32, (n_rows, GROUP), 1)
    q8 = jnp.broadcast_to(q_ref[...].astype(F32), (n_rows, GROUP))
    qm = jnp.where(lane // DH_DIFF == row, q8, 0.0).astype(BF16)
    slopes = _alibi_slopes(H_DIFF)
    row1 = lax.broadcasted_iota(jnp.int32, (n_rows, 1), 0)
    slope_col = jnp.zeros((n_rows, 1), F32)
    for h in range(H_DIFF):
        slope_col = jnp.where(row1 // 2 == h, float(slopes[h]), slope_col)

    @pl.when(step == 0)
    def _():
        m_scr[...] = jnp.sum(qm.astype(F32) * ks_ref[...], axis=-1, keepdims=True)
        l_scr[...] = jnp.ones_like(l_scr)
        acc_scr[...] = jnp.broadcast_to(vs_ref[...], (n_rows, GROUP))

    s = jnp.concatenate([_dot_nt(qm, k_refs[pg][...].astype(BF16)) for pg in range(n_pg)], axis=1)
    kpos = step * (n_pg * page) + lax.broadcasted_iota(jnp.int32, (1, n_pg * page), 1)
    s = s - slope_col * (past - kpos).astype(F32)
    m_old = m_scr[...]
    m_new = jnp.maximum(m_old, jnp.max(s, axis=-1, keepdims=True))
    alpha = jnp.exp(m_old - m_new)
    pe = jnp.exp(s - m_new)
    l_scr[...] = alpha * l_scr[...] + jnp.sum(pe, axis=-1, keepdims=True)
    pv = _dot(pe[:, :page].astype(BF16), v_refs[0][...].astype(BF16))
    for pg in range(1, n_pg):
        pv = pv + _dot(pe[:, pg * page:(pg + 1) * page].astype(BF16), v_refs[pg][...].astype(BF16))
    acc_scr[...] = alpha * acc_scr[...] + pv
    m_scr[...] = m_new

    @pl.when(step == pl.num_programs(1) - 1)
    def _():
        lp = lamp_ref[...]
        lam = (jnp.exp(jnp.sum(lp[0:1] * lp[1:2], axis=-1, keepdims=True))
               - jnp.exp(jnp.sum(lp[2:3] * lp[3:4], axis=-1, keepdims=True)) + lam_init)
        a = acc_scr[...] / l_scr[...]
        coef = jnp.where(row % 2 == 0, 1.0, -lam)
        a = jnp.where(lane // DV_DIFF == row // 2, a * coef, 0.0)
        o = jnp.sum(a, axis=0, keepdims=True)
        g = g_ref[...]
        parts = [_rms(o[:, h * DV_DIFF:(h + 1) * DV_DIFF], g) for h in range(H_DIFF)]
        o_ref[...] = jnp.concatenate(parts, axis=1) * (1.0 - lam_init)


def _diff_decode(page_table, q_bf, k_self, v_self, lamp, g_subln, cache_k, cache_v, *,
                 layer_off, n_pg, lam_init):
    nb, n_pages = page_table.shape
    page = cache_k.shape[1]
    past = n_pages * page
    assert n_pages % n_pg == 0

    def page_spec(pg):
        return pl.BlockSpec((None, page, GROUP),
                            lambda b, s, pt: (layer_off + pt[b * n_pages + s * n_pg + pg], 0, 0))

    vec_spec = pl.BlockSpec((None, 1, GROUP), lambda b, s, pt: (b, 0, 0))
    grid_spec = pltpu.PrefetchScalarGridSpec(
        num_scalar_prefetch=1,
        grid=(nb, n_pages // n_pg),
        in_specs=[vec_spec, vec_spec, vec_spec,
                  pl.BlockSpec((4, DH_DIFF), lambda b, s, pt: (0, 0)),
                  pl.BlockSpec((1, DV_DIFF), lambda b, s, pt: (0, 0))]
                 + [page_spec(pg) for pg in range(n_pg)] * 2,
        out_specs=vec_spec,
        scratch_shapes=[pltpu.VMEM((2 * H_DIFF, 1), F32), pltpu.VMEM((2 * H_DIFF, 1), F32),
                        pltpu.VMEM((2 * H_DIFF, GROUP), F32)],
    )
    out = pl.pallas_call(
        functools.partial(_diff_decode_kernel, n_pg=n_pg, past=past, lam_init=lam_init),
        grid_spec=grid_spec,
        out_shape=jax.ShapeDtypeStruct((nb, 1, GROUP), F32),
        compiler_params=_params(("arbitrary", "arbitrary")),
    )(page_table.reshape(-1), q_bf.reshape(nb, 1, GROUP), k_self.reshape(nb, 1, GROUP),
      v_self.reshape(nb, 1, GROUP), lamp, g_subln.reshape(1, DV_DIFF),
      *([cache_k] * n_pg), *([cache_v] * n_pg))
    return out.reshape(nb, GROUP)


def _moba_select_kernel(pt_ref, q_ref, *rest, n_pg, n_blk):
    k_refs = rest[:n_pg]
    sel_ref, km_scr = rest[n_pg:]
    step = pl.program_id(1)
    page = k_refs[0].shape[0]
    pg_per_blk = MOBA_BLOCK // page
    blk_per_step = n_pg // pg_per_blk

    @pl.when(step == 0)
    def _():
        km_scr[...] = jnp.zeros_like(km_scr)

    for r in range(blk_per_step):
        tot = jnp.sum(k_refs[r * pg_per_blk][...], axis=0, keepdims=True)
        for pg in range(1, pg_per_blk):
            tot = tot + jnp.sum(k_refs[r * pg_per_blk + pg][...], axis=0, keepdims=True)
        km_scr[pl.ds(step * blk_per_step + r, 1), :] = tot * (1.0 / MOBA_BLOCK)

    @pl.when(step == pl.num_programs(1) - 1)
    def _():
        row = lax.broadcasted_iota(jnp.int32, (H_MOBA, GROUP), 0)
        lane = lax.broadcasted_iota(jnp.int32, (H_MOBA, GROUP), 1)
        q8 = jnp.broadcast_to(q_ref[...], (H_MOBA, GROUP)).astype(F32)
        qm = jnp.where(lane // DH_MOBA == row, q8, 0.0)
        gate = _dot_nt(qm, km_scr[...], precision=lax.Precision.HIGHEST)
        blk = lax.broadcasted_iota(jnp.int32, (H_MOBA, LANES), 1).astype(F32)
        g = jnp.where(blk < n_blk, gate, NEG_INF)
        out = jnp.zeros((H_MOBA, LANES), F32)
        for r in range(MOBA_TOPK):
            mx = jnp.max(g, axis=-1, keepdims=True)
            idx = jnp.min(jnp.where(g == mx, blk, float(LANES)), axis=-1, keepdims=True)
            out = jnp.where(blk == r, idx, out)
            g = jnp.where(blk == idx, REMOVED, g)
        sel_ref[...] = out.astype(jnp.int32)


def _moba_select(page_table, q_f32, cache_k, *, layer_off, n_pg):
    nb, n_pages = page_table.shape
    page = cache_k.shape[1]
    n_blk = n_pages * page // MOBA_BLOCK
    assert n_blk <= LANES and n_blk >= MOBA_TOPK and n_pages % n_pg == 0

    def page_spec(pg):
        return pl.BlockSpec((None, page, GROUP),
                            lambda b, s, pt: (layer_off + pt[b * n_pages + s * n_pg + pg], 0, 0))

    grid_spec = pltpu.PrefetchScalarGridSpec(
        num_scalar_prefetch=1,
        grid=(nb, n_pages // n_pg),
        in_specs=[pl.BlockSpec((None, 1, GROUP), lambda b, s, pt: (b, 0, 0))]
                 + [page_spec(pg) for pg in range(n_pg)],
        out_specs=pl.BlockSpec((None, H_MOBA, LANES), lambda b, s, pt: (b, 0, 0)),
        scratch_shapes=[pltpu.VMEM((LANES, GROUP), F32)],
    )
    return pl.pallas_call(
        functools.partial(_moba_select_kernel, n_pg=n_pg, n_blk=n_blk),
        grid_spec=grid_spec,
        out_shape=jax.ShapeDtypeStruct((nb, H_MOBA, LANES), jnp.int32),
        compiler_params=_params(("arbitrary", "arbitrary")),
    )(page_table.reshape(-1), q_f32.reshape(nb, 1, GROUP), *([cache_k] * n_pg))


def _moba_decode_kernel(pt_ref, sel_ref, slope_ref, q_ref, ks_ref, vs_ref, *rest, n_kv, past):
    k_refs = rest[:n_kv]
    v_refs = rest[n_kv:2 * n_kv]
    o_ref = rest[2 * n_kv]
    b = pl.program_id(0)
    pr = pl.program_id(1)
    page = k_refs[0].shape[0]
    pg_per_blk = MOBA_BLOCK // page
    per_head = MOBA_TOPK * pg_per_blk
    sub = 8
    lane = lax.broadcasted_iota(jnp.int32, (sub, LANES), 1)
    lane_k = lax.broadcasted_iota(jnp.int32, (1, page), 1)
    q8 = jnp.broadcast_to(q_ref[...], (sub, LANES))
    outs = []
    for hf in range(2):
        h = 2 * pr + hf
        mine = (lane // DH_MOBA) == hf
        qh = jnp.where(mine, q8, jnp.zeros_like(q8))
        slope = slope_ref[h]
        s_self = jnp.sum(qh.astype(F32) * ks_ref[...], axis=-1, keepdims=True)
        scores = []
        for r in range(MOBA_TOPK):
            blk = sel_ref[(b * H_MOBA + h) * MOBA_TOPK + r]
            for pg in range(pg_per_blk):
                kk = k_refs[hf * per_head + r * pg_per_blk + pg][...].astype(BF16)
                kpos = blk * MOBA_BLOCK + pg * page + lane_k
                scores.append(_dot_nt(qh, kk) - slope * (past - kpos).astype(F32))
        m = s_self
        for s in scores:
            m = jnp.maximum(m, jnp.max(s, axis=-1, keepdims=True))
        p_self = jnp.exp(s_self - m)
        den = p_self
        acc = p_self * vs_ref[...]
        for idx, s in enumerate(scores):
            pe = jnp.exp(s - m)
            den = den + jnp.sum(pe, axis=-1, keepdims=True)
            acc = acc + _dot(pe.astype(BF16), v_refs[hf * per_head + idx][...].astype(BF16))
        outs.append(acc / den)
    o_ref[...] = jnp.where(lane // DH_MOBA == 0, outs[0], outs[1])[0:1]


def _moba_decode(page_table, sel, q_bf, k_self, v_self, cache_k, cache_v, *, layer_off):
    nb, n_pages = page_table.shape
    page = cache_k.shape[1]
    past = n_pages * page
    pg_per_blk = MOBA_BLOCK // page
    n_pair = GROUP // LANES
    n_kv = 2 * MOBA_TOPK * pg_per_blk

    def page_spec(hf, r, pg):
        def index_map(b, pr, pt, sl):
            blk = sl[(b * H_MOBA + 2 * pr + hf) * MOBA_TOPK + r]
            return (layer_off + pt[b * n_pages + blk * pg_per_blk + pg], 0, pr)
        return pl.BlockSpec((None, page, LANES), index_map)

    kv_specs = [page_spec(hf, r, pg) for hf in range(2) for r in range(MOBA_TOPK)
                for pg in range(pg_per_blk)]
    vec_spec = pl.BlockSpec((None, 1, LANES), lambda b, pr, pt, sl: (b, 0, pr))
    grid_spec = pltpu.PrefetchScalarGridSpec(
        num_scalar_prefetch=2,
        grid=(nb, n_pair),
        in_specs=[pl.BlockSpec(memory_space=pltpu.SMEM), vec_spec, vec_spec, vec_spec]
                 + kv_specs * 2,
        out_specs=vec_spec,
    )
    out = pl.pallas_call(
        functools.partial(_moba_decode_kernel, n_kv=n_kv, past=past),
        grid_spec=grid_spec,
        out_shape=jax.ShapeDtypeStruct((nb, 1, GROUP), F32),
        compiler_params=_params(("arbitrary", "arbitrary")),
    )(page_table.reshape(-1), sel.reshape(-1), jnp.asarray(_alibi_slopes(H_MOBA)),
      q_bf.reshape(nb, 1, GROUP), k_self.reshape(nb, 1, GROUP), v_self.reshape(nb, 1, GROUP),
      *([cache_k] * n_kv), *([cache_v] * n_kv))
    return out.reshape(nb, GROUP)


PROMPT_TILE = 512
ATTN_TILE = MOBA_BLOCK
DECODE_PAGES = 8


def kernel(x_prompt, x_sample, cache_diff_k, cache_diff_v, cache_moba_k, cache_moba_v, cache_mem_k, cache_mem_v, page_table, mem_prompt, g_pre_mix, w_in, lambda_q1, lambda_k1, lambda_q2, lambda_k2, g_subln, w_out, g_post_mix, g_mem, w_mem_k, w_mem_v, g_pre_x, w_xq, w_xo, g_post_x, g_pre_ff, w_up, w_down, g_post_ff):
    nb_p, seq, d = x_prompt.shape
    nb_s, s_len, _ = x_sample.shape
    assert s_len == 1, "the sample kernels handle one new token per sequence"
    depth = w_in.shape[0]
    n_pool, page = cache_diff_k.shape[1], cache_diff_k.shape[2]
    assert MOBA_BLOCK % page == 0 and (page_table.shape[1] * page) % MOBA_BLOCK == 0
    n_blk_seq = seq // MOBA_BLOCK
    tm = min(PROMPT_TILE, seq)

    xp = x_prompt.reshape(nb_p * seq, d)
    xs = x_sample.reshape(nb_s, d)
    ck_d = cache_diff_k.reshape(depth * n_pool, page, GROUP)
    cv_d = cache_diff_v.reshape(depth * n_pool, page, GROUP)
    ck_m = cache_moba_k.reshape(depth * n_pool, page, GROUP)
    cv_m = cache_moba_v.reshape(depth * n_pool, page, GROUP)
    mem_x = mem_prompt.reshape(nb_p * N_MEM, d)
    slopes_d = _alibi_slopes(H_DIFF)
    slopes_m = _alibi_slopes(H_MOBA)

    outs = [[] for _ in range(10)]
    for l in range(depth):
        lam_init = 0.8 - 0.6 * math.exp(-0.3 * l)
        lamp = jnp.stack([lambda_q1[l], lambda_k1[l], lambda_q2[l], lambda_k2[l]]).astype(F32)
        w_in_bf = w_in[l].astype(BF16)
        w_out_bf = w_out[l].astype(BF16)
        wts = {
            'w_out_a': w_out_bf[:GROUP], 'w_out_b': w_out_bf[GROUP:], 'g_post_mix': g_post_mix[l],
            'g_pre_x': g_pre_x[l], 'w_xq': w_xq[l].astype(BF16), 'w_xo': w_xo[l].astype(BF16),
            'g_post_x': g_post_x[l], 'g_pre_ff': g_pre_ff[l], 'w_up': w_up[l].astype(BF16),
            'w_down': w_down[l].astype(BF16), 'g_post_ff': g_post_ff[l],
        }

        (dq, dk, dk_bf, dv, dv_bf, mq, mk, mk_bf, mv, mv_bf, sel) = _project(
            xp, g_pre_mix[l], w_in_bf, tm=tm, n_blk_seq=n_blk_seq)
        o_d = _pair_attention(dq, dk_bf, dv_bf, slopes_d, nb=nb_p, seq=seq, tq=ATTN_TILE, moba=False,
                              lamp=lamp, g_subln=g_subln[l], lam_init=lam_init)
        o_m = _pair_attention(mq, mk_bf, mv_bf, slopes_m, nb=nb_p, seq=seq, tq=ATTN_TILE, moba=True,
                              sel=sel)
        mem_tm = min(PROMPT_TILE, nb_p * N_MEM)
        memk = _norm_matmul(mem_x, g_mem[l], w_mem_k[l].astype(BF16), tm=mem_tm, out_dtype=F32)
        memv = _norm_matmul(mem_x, g_mem[l], w_mem_v[l].astype(BF16), tm=mem_tm, out_dtype=F32)
        xp = _finish_layer(xp, o_d, o_m, memk.reshape(nb_p, N_MEM, -1), memv.reshape(nb_p, N_MEM, -1),
                           wts, nb=nb_p, rows=seq, tm=tm)
        outs[0].append(dk.reshape(nb_p, seq, H_DIFF, 2, DH_DIFF))
        outs[1].append(dv.reshape(nb_p, seq, H_DIFF, DV_DIFF))
        outs[2].append(mk.reshape(nb_p, seq, H_MOBA, DH_MOBA))
        outs[3].append(mv.reshape(nb_p, seq, H_MOBA, DH_MOBA))
        outs[4].append(memk.reshape(nb_p, N_MEM, H_MEM, DH_MEM))
        outs[5].append(memv.reshape(nb_p, N_MEM, H_MEM, DH_MEM))

        (sq, sk, _, sv, _, tq_, tk, _, tv, _, tq_f32) = _project(xs, g_pre_mix[l], w_in_bf, tm=nb_s)
        o_ds = _diff_decode(page_table, sq, sk, sv, lamp, g_subln[l], ck_d, cv_d,
                            layer_off=l * n_pool, n_pg=DECODE_PAGES, lam_init=lam_init)
        sel_s = _moba_select(page_table, tq_f32, ck_m, layer_off=l * n_pool, n_pg=DECODE_PAGES)
        o_ms = _moba_decode(page_table, sel_s[:, :, :MOBA_TOPK], tq_, tk, tv, ck_m, cv_m,
                            layer_off=l * n_pool)
        xs = _finish_layer(xs, o_ds, o_ms, cache_mem_k[l].reshape(nb_s, N_MEM, -1),
                           cache_mem_v[l].reshape(nb_s, N_MEM, -1), wts, nb=nb_s, rows=1, tm=nb_s,
                           rows_pad=8)
        outs[6].append(sk.reshape(nb_s, 1, H_DIFF, 2, DH_DIFF))
        outs[7].append(sv.reshape(nb_s, 1, H_DIFF, DV_DIFF))
        outs[8].append(tk.reshape(nb_s, 1, H_MOBA, DH_MOBA))
        outs[9].append(tv.reshape(nb_s, 1, H_MOBA, DH_MOBA))

    return (xp.reshape(nb_p, seq, d), xs.reshape(nb_s, 1, d), *[jnp.stack(o) for o in outs])
```

```python
import functools
import math

import numpy as np
import jax
import jax.numpy as jnp
from jax import lax
from jax.experimental import pallas as pl
from jax.experimental.pallas import tpu as pltpu

F32 = jnp.float32
BF16 = jnp.bfloat16

EPS = 1e-6
NEG_INF = -1e30
REMOVED = -3e38
LOG2E = 1.4426950408889634

H_DIFF = 4
DH_DIFF = 64
DV_DIFF = 2 * DH_DIFF
H_MOBA = 8
DH_MOBA = 64
MOBA_BLOCK = 256
MOBA_TOPK = 3
H_MEM = 4
DH_MEM = 128
N_MEM = 256
GROUP = 512
LANES = 128
SUBLANES = 8
HALF = LANES // 2
SEL_STRIDE = LANES // H_MOBA

AUG_SEL = 0
AUG_POS_IN = 16
AUG_POS_BLK = 19
N_SPLIT = 3

VMEM_LIMIT_BYTES = 56 * 1024 * 1024


def _alibi_slopes(n):
    return np.power(2.0, -8.0 * np.arange(1, n + 1) / n).astype(np.float32)


def _split_bf16(x):
    parts, rest = [], np.asarray(x, np.float32)
    for _ in range(N_SPLIT):
        p = rest.astype(BF16).astype(np.float32)
        parts.append(p)
        rest = rest - p
    return parts


def _query_aug_table(slopes_per_half):
    slopes_per_half = np.asarray(slopes_per_half, np.float32)
    tab = np.zeros((slopes_per_half.shape[0], 1, LANES), np.float32)
    for hf in range(2):
        for i, part in enumerate(_split_bf16(slopes_per_half[:, hf] * np.float32(LOG2E))):
            tab[:, 0, hf * HALF + AUG_POS_IN + i] = part
            tab[:, 0, hf * HALF + AUG_POS_BLK + i] = part
    return tab


def _key_aug_table(seq):
    t = np.arange(seq)
    tab = np.zeros((seq, LANES), np.float32)
    for hf in range(2):
        tab[t, hf * HALF + AUG_SEL + t // MOBA_BLOCK] = 1.0
        tab[:, hf * HALF + AUG_POS_IN:hf * HALF + AUG_POS_IN + N_SPLIT] = (t % MOBA_BLOCK)[:, None]
        tab[:, hf * HALF + AUG_POS_BLK:hf * HALF + AUG_POS_BLK + N_SPLIT] = (t - t % MOBA_BLOCK)[:, None]
    return jnp.asarray(tab, BF16)


def _params(sem):
    return pltpu.CompilerParams(dimension_semantics=sem, vmem_limit_bytes=VMEM_LIMIT_BYTES)


def _rms(x, g):
    return x * lax.rsqrt(jnp.mean(x * x, axis=-1, keepdims=True) + EPS) * g


def _dot(a, b):
    return jnp.dot(a, b, preferred_element_type=F32)


def _dot_nt(a, b, precision=None):
    return lax.dot_general(a, b, (((1,), (1,)), ((), ())), preferred_element_type=F32,
                           precision=precision)


def _lambda(lp, lam_init):
    return (jnp.exp(jnp.sum(lp[0:1] * lp[1:2], axis=-1, keepdims=True))
            - jnp.exp(jnp.sum(lp[2:3] * lp[3:4], axis=-1, keepdims=True)) + lam_init)


def _proj_prompt_kernel(x_ref, g_ref, w_ref, atab_ref, dq_ref, dk_ref, dkb_ref, dv_ref, dvb_ref,
                        mq_ref, mk_ref, mkb_ref, mv_ref, mvb_ref, aug_ref, kmt_ref, *, tm, n_blk_seq):
    xn = _rms(x_ref[...], g_ref[...]).astype(BF16)

    def col(c):
        return _dot(xn, w_ref[:, c * GROUP:(c + 1) * GROUP])

    dq_ref[...] = (col(0) * (DH_DIFF ** -0.5 * LOG2E)).astype(BF16)
    z = col(1)
    dk_ref[...] = z
    dkb_ref[...] = z.astype(BF16)
    z = col(2)
    dv_ref[...] = z
    dvb_ref[...] = z.astype(BF16)
    mq = col(3)
    mq_ref[...] = (mq * (DH_MOBA ** -0.5 * LOG2E)).astype(BF16)
    mk = col(4)
    mk_ref[...] = mk
    mkb_ref[...] = mk.astype(BF16)
    z = col(5)
    mv_ref[...] = z
    mvb_ref[...] = z.astype(BF16)

    t = pl.program_id(0)
    blk_per_tile = tm // MOBA_BLOCK
    tiles_per_seq = n_blk_seq // blk_per_tile
    blk0 = (t % tiles_per_seq) * blk_per_tile

    @pl.when(t == 0)
    def _():
        kmt_ref[...] = jnp.zeros_like(kmt_ref)

    lane_g = lax.broadcasted_iota(jnp.int32, (1, GROUP), 1) // DH_MOBA
    for r in range(blk_per_tile):
        km = jnp.mean(mk[r * MOBA_BLOCK:(r + 1) * MOBA_BLOCK], axis=0, keepdims=True)
        for h in range(H_MOBA):
            kmt_ref[pl.ds(h * SEL_STRIDE + blk0 + r, 1), :] = jnp.where(lane_g == h, km, 0.0)

    gate = _dot_nt(mq * (DH_MOBA ** -0.5), kmt_ref[...], precision=lax.Precision.HIGHEST)
    lane = lax.broadcasted_iota(jnp.int32, (tm, LANES), 1)
    row = lax.broadcasted_iota(jnp.int32, (tm, LANES), 0)
    n_in = lane % SEL_STRIDE
    own = blk0 + row // MOBA_BLOCK
    past = n_in < own
    g = jnp.where(past, gate, NEG_INF)
    cnt = jnp.zeros((tm, LANES), jnp.int32)
    for s in range(1, n_blk_seq):
        lo = pltpu.roll(g, s, 1)
        cnt += jnp.where((n_in >= s) & (lo >= g), 1, 0)
        hi = pltpu.roll(g, LANES - s, 1)
        cnt += jnp.where((n_in + s < SEL_STRIDE) & (hi > g), 1, 0)
    visible = (past & (cnt < MOBA_TOPK)) | (n_in == own)
    sel = jnp.where(visible, 0.0, NEG_INF).astype(BF16)
    er = lax.broadcasted_iota(jnp.int32, (LANES, GROUP), 0)
    ec = lax.broadcasted_iota(jnp.int32, (LANES, GROUP), 1)
    spread = jnp.where((ec // HALF == er // SEL_STRIDE) & (ec % HALF == AUG_SEL + er % SEL_STRIDE),
                       1.0, 0.0).astype(BF16)
    aug_ref[...] = (_dot(sel, spread) + atab_ref[...]).astype(BF16)


def _project_prompt(x2d, g, w_bf, aug_tab, *, tm, n_blk_seq):
    n, d = x2d.shape
    assert tm % MOBA_BLOCK == 0 and n_blk_seq % (tm // MOBA_BLOCK) == 0 and n_blk_seq <= SEL_STRIDE
    row_spec = pl.BlockSpec((tm, GROUP), lambda t: (t, 0))
    f32_out = jax.ShapeDtypeStruct((n, GROUP), F32)
    bf_out = jax.ShapeDtypeStruct((n, GROUP), BF16)
    return pl.pallas_call(
        functools.partial(_proj_prompt_kernel, tm=tm, n_blk_seq=n_blk_seq),
        grid=(n // tm,),
        in_specs=[pl.BlockSpec((tm, d), lambda t: (t, 0)),
                  pl.BlockSpec((1, d), lambda t: (0, 0)),
                  pl.BlockSpec((d, 6 * GROUP), lambda t: (0, 0)),
                  pl.BlockSpec((1, GROUP), lambda t: (0, 0))],
        out_specs=[row_spec] * 11,
        out_shape=[bf_out, f32_out, bf_out, f32_out, bf_out, bf_out, f32_out, bf_out, f32_out, bf_out, bf_out],
        scratch_shapes=[pltpu.VMEM((LANES, GROUP), F32)],
        compiler_params=_params(("arbitrary",)),
    )(x2d, g.reshape(1, d), w_bf, aug_tab)


def _proj_sample_kernel(x_ref, g_ref, w_ref, dq_ref, dk_ref, dv_ref, mq_ref, mk_ref, mv_ref):
    xn = _rms(x_ref[...], g_ref[...]).astype(BF16)
    scales = (DH_DIFF ** -0.5, 1.0, 1.0, DH_MOBA ** -0.5, 1.0, 1.0)
    for c, (ref, sc) in enumerate(zip((dq_ref, dk_ref, dv_ref, mq_ref, mk_ref, mv_ref), scales)):
        ref[...] = _dot(xn, w_ref[:, c * GROUP:(c + 1) * GROUP]) * sc


def _project_sample(x2d, g, w_bf):
    n, d = x2d.shape
    out = jax.ShapeDtypeStruct((n, GROUP), F32)
    return pl.pallas_call(
        _proj_sample_kernel,
        grid=(1,),
        in_specs=[pl.BlockSpec((n, d), lambda t: (0, 0)),
                  pl.BlockSpec((1, d), lambda t: (0, 0)),
                  pl.BlockSpec((d, 6 * GROUP), lambda t: (0, 0))],
        out_specs=[pl.BlockSpec((n, GROUP), lambda t: (0, 0))] * 6,
        out_shape=[out] * 6,
        compiler_params=_params(("arbitrary",)),
    )(x2d, g.reshape(1, d), w_bf)


def _pair_attn_kernel(q_ref, aug_ref, k_ref, ktab_ref, v_ref, *rest, tq, moba, lam_init):
    if moba:
        o_ref, m_scr, acc_scr = rest
    else:
        lamp_ref, g_ref, o_ref, m_scr, acc_scr = rest
    i = pl.program_id(2)

    q = q_ref[...]
    if moba:
        aug = aug_ref[...]
    else:
        aug = jnp.broadcast_to(aug_ref[...], (tq, LANES)).astype(BF16)
    lane = lax.broadcasted_iota(jnp.int32, (tq, LANES), 1)
    lo = lane < HALF
    zero = jnp.zeros_like(q)
    q2 = jnp.concatenate(
        [jnp.concatenate([jnp.where(lo, q, zero), jnp.where(lo, aug, zero)], axis=1),
         jnp.concatenate([jnp.where(lo, zero, q), jnp.where(lo, zero, aug)], axis=1)], axis=0)

    m_scr[...] = jnp.full_like(m_scr, NEG_INF)
    acc_scr[...] = jnp.zeros_like(acc_scr)
    ones = jnp.ones((tq, LANES), BF16)
    n_rep = tq // LANES

    def block(jb, diag):
        start = pl.multiple_of(jb * tq, tq)
        kj = jnp.concatenate([k_ref[pl.ds(start, tq), :], ktab_ref[pl.ds(start, tq), :]], axis=1)
        vj = jnp.concatenate([v_ref[pl.ds(start, tq), :], ones], axis=1)
        s = _dot_nt(q2, kj)
        if diag:
            r = lax.broadcasted_iota(jnp.int32, (2 * tq, tq), 0)
            c = lax.broadcasted_iota(jnp.int32, (2 * tq, tq), 1)
            s = jnp.where(r % tq >= c, s, NEG_INF)
        m_old = m_scr[...]
        m_new = jnp.maximum(m_old, jnp.max(s, axis=-1, keepdims=True))
        alpha = jnp.exp2(m_old - m_new)
        pe = jnp.exp2(s - jnp.concatenate([m_new] * n_rep, axis=1))
        acc_scr[...] = jnp.concatenate([alpha, alpha], axis=1) * acc_scr[...] + _dot(pe.astype(BF16), vj)
        m_scr[...] = m_new

    block(i, True)

    def body(jb, carry):
        block(jb, False)
        return carry

    lax.fori_loop(0, i, body, 0)

    acc = acc_scr[...]
    o2 = acc[:, :LANES] / acc[:, LANES:]
    o0, o1 = o2[:tq], o2[tq:]
    if moba:
        o_ref[...] = jnp.where(lo, o0, o1).astype(o_ref.dtype)
    else:
        o = o0 - _lambda(lamp_ref[...], lam_init) * o1
        o_ref[...] = (_rms(o, g_ref[...]) * (1.0 - lam_init)).astype(o_ref.dtype)


def _pair_attention(q_bf, aug, k_bf, ktab, v_bf, *, nb, seq, tq, moba, lamp=None, g_subln=None,
                    lam_init=0.0):
    n_pair = GROUP // LANES
    nq = seq // tq
    assert tq % MOBA_BLOCK == 0 and seq % tq == 0
    q_spec = pl.BlockSpec((tq, LANES), lambda b, p, i: (b * nq + i, p))
    kv_spec = pl.BlockSpec((seq, LANES), lambda b, p, i: (b, p))
    if moba:
        aug_spec = q_spec
    else:
        aug_spec = pl.BlockSpec((None, 1, LANES), lambda b, p, i: (p, 0, 0))
    in_specs = [q_spec, aug_spec, kv_spec, pl.BlockSpec((seq, LANES), lambda b, p, i: (0, 0)), kv_spec]
    args = [q_bf, aug, k_bf, ktab, v_bf]
    if not moba:
        in_specs += [pl.BlockSpec((4, DH_DIFF), lambda b, p, i: (0, 0)),
                     pl.BlockSpec((1, DV_DIFF), lambda b, p, i: (0, 0))]
        args += [lamp, g_subln.reshape(1, DV_DIFF)]
    return pl.pallas_call(
        functools.partial(_pair_attn_kernel, tq=tq, moba=moba, lam_init=lam_init),
        grid=(nb, n_pair, nq),
        in_specs=in_specs,
        out_specs=q_spec,
        out_shape=jax.ShapeDtypeStruct((nb * seq, GROUP), BF16),
        scratch_shapes=[pltpu.VMEM((2 * tq, LANES), F32), pltpu.VMEM((2 * tq, 2 * LANES), F32)],
        compiler_params=_params(("arbitrary", "arbitrary", "arbitrary")),
    )(*args)


def _norm_mm_kernel(x_ref, g_ref, w_ref, o_ref, *, scale, relu2):
    y = _dot(_rms(x_ref[...], g_ref[...]).astype(BF16), w_ref[...])
    if relu2:
        y = jnp.square(jnp.maximum(y, 0.0))
    if scale != 1.0:
        y = y * scale
    o_ref[...] = y.astype(o_ref.dtype)


def _norm_matmul(x2d, g, w_bf, *, tm, out_dtype, scale=1.0, relu2=False):
    n, d = x2d.shape
    e = w_bf.shape[1]
    return pl.pallas_call(
        functools.partial(_norm_mm_kernel, scale=scale, relu2=relu2),
        grid=(n // tm,),
        in_specs=[pl.BlockSpec((tm, d), lambda t: (t, 0)),
                  pl.BlockSpec((1, d), lambda t: (0, 0)),
                  pl.BlockSpec((d, e), lambda t: (0, 0))],
        out_specs=pl.BlockSpec((tm, e), lambda t: (t, 0)),
        out_shape=jax.ShapeDtypeStruct((n, e), out_dtype),
        compiler_params=_params(("arbitrary",)),
    )(x2d, g.reshape(1, d), w_bf)


def _mm_norm_res_kernel(*refs, n_in):
    a_refs = refs[:n_in]
    w_refs = refs[n_in:2 * n_in]
    x_ref, g_ref, o_ref = refs[2 * n_in:]
    y = _dot(a_refs[0][...].astype(BF16), w_refs[0][...])
    for a_ref, w_ref in zip(a_refs[1:], w_refs[1:]):
        y = y + _dot(a_ref[...].astype(BF16), w_ref[...])
    o_ref[...] = x_ref[...] + _rms(y, g_ref[...])


def _matmul_norm_residual(a_list, w_list, x2d, g, *, tm):
    n, d = x2d.shape
    n_in = len(a_list)
    in_specs = [pl.BlockSpec((tm, a.shape[1]), lambda t: (t, 0)) for a in a_list]
    in_specs += [pl.BlockSpec(w.shape, lambda t: (0, 0)) for w in w_list]
    in_specs += [pl.BlockSpec((tm, d), lambda t: (t, 0)), pl.BlockSpec((1, d), lambda t: (0, 0))]
    return pl.pallas_call(
        functools.partial(_mm_norm_res_kernel, n_in=n_in),
        grid=(n // tm,),
        in_specs=in_specs,
        out_specs=pl.BlockSpec((tm, d), lambda t: (t, 0)),
        out_shape=jax.ShapeDtypeStruct((n, d), F32),
        compiler_params=_params(("arbitrary",)),
    )(*a_list, *w_list, x2d, g.reshape(1, d))


def _mem_attn_kernel(q_ref, k_ref, v_ref, o_ref):
    q = q_ref[...]
    outs = []
    for h in range(H_MEM):
        cols = slice(h * DH_MEM, (h + 1) * DH_MEM)
        s = _dot_nt(q[:, cols], k_ref[:, cols].astype(BF16))
        pe = jnp.exp(s - jnp.max(s, axis=-1, keepdims=True))
        o = _dot(pe.astype(BF16), v_ref[:, cols].astype(BF16))
        outs.append(o / jnp.sum(pe, axis=-1, keepdims=True))
    o_ref[...] = jnp.concatenate(outs, axis=1).astype(o_ref.dtype)


def _mem_attention(q_bf, mem_k, mem_v, *, nb, rows, tm):
    d = H_MEM * DH_MEM
    nt = rows // tm
    return pl.pallas_call(
        _mem_attn_kernel,
        grid=(nb, nt),
        in_specs=[pl.BlockSpec((tm, d), lambda b, t: (b * nt + t, 0)),
                  pl.BlockSpec((None, N_MEM, d), lambda b, t: (b, 0, 0)),
                  pl.BlockSpec((None, N_MEM, d), lambda b, t: (b, 0, 0))],
        out_specs=pl.BlockSpec((tm, d), lambda b, t: (b * nt + t, 0)),
        out_shape=jax.ShapeDtypeStruct((nb * rows, d), BF16),
        compiler_params=_params(("arbitrary", "arbitrary")),
    )(q_bf, mem_k, mem_v)


def _finish_layer(x2d, o_diff, o_moba, mem_k, mem_v, wts, *, nb, rows, tm, rows_pad=1):
    x1 = _matmul_norm_residual([o_diff, o_moba], [wts['w_out_a'], wts['w_out_b']], x2d,
                               wts['g_post_mix'], tm=tm)
    q = _norm_matmul(x1, wts['g_pre_x'], wts['w_xq'], tm=tm, out_dtype=BF16, scale=DH_MEM ** -0.5)
    if rows_pad > 1:
        qp = jnp.broadcast_to(q[:, None, :], (nb * rows, rows_pad, q.shape[1])).reshape(-1, q.shape[1])
        c = _mem_attention(qp, mem_k, mem_v, nb=nb, rows=rows * rows_pad, tm=rows * rows_pad)
        c = c.reshape(nb * rows, rows_pad, -1)[:, 0, :]
    else:
        c = _mem_attention(q, mem_k, mem_v, nb=nb, rows=rows, tm=min(tm, rows))
    x2 = _matmul_norm_residual([c], [wts['w_xo']], x1, wts['g_post_x'], tm=tm)
    hdn = _norm_matmul(x2, wts['g_pre_ff'], wts['w_up'], tm=tm, out_dtype=BF16, relu2=True)
    return _matmul_norm_residual([hdn], [wts['w_down']], x2, wts['g_post_ff'], tm=tm)


def _query_columns(q_row):
    r = lax.broadcasted_iota(jnp.int32, (GROUP, GROUP), 0)
    c = lax.broadcasted_iota(jnp.int32, (GROUP, GROUP), 1)
    col = jnp.sum(jnp.where(r == c, jnp.broadcast_to(q_row, (GROUP, GROUP)), 0.0), axis=-1, keepdims=True)
    return jnp.broadcast_to(col, (GROUP, LANES))


def _page_scores(kt, qb, n_grp):
    prod = kt * qb
    w = GROUP // n_grp
    return jnp.concatenate([jnp.sum(prod[g * w:(g + 1) * w], axis=0, keepdims=True) for g in range(n_grp)],
                           axis=0)


def _self_scores(q_row, k_row, n_grp):
    row = lax.broadcasted_iota(jnp.int32, (n_grp, GROUP), 0)
    lane = lax.broadcasted_iota(jnp.int32, (n_grp, GROUP), 1)
    prod = jnp.broadcast_to(q_row * k_row, (n_grp, GROUP))
    return jnp.sum(jnp.where(lane // (GROUP // n_grp) == row, prod, 0.0), axis=-1, keepdims=True)


def _diff_decode_kernel(pt_ref, q_ref, ks_ref, vs_ref, lamp_ref, g_ref, *rest, n_pg, past, lam_init):
    k_refs = rest[:n_pg]
    v_refs = rest[n_pg:2 * n_pg]
    o_ref, qb_scr, m_scr, l_scr, acc_scr = rest[2 * n_pg:]
    step = pl.program_id(1)
    n_rows = 2 * H_DIFF
    page = k_refs[0].shape[1]
    slopes = _alibi_slopes(H_DIFF)
    row1 = lax.broadcasted_iota(jnp.int32, (n_rows, 1), 0)
    slope_col = jnp.zeros((n_rows, 1), F32)
    for h in range(H_DIFF):
        slope_col = jnp.where(row1 // 2 == h, float(slopes[h]), slope_col)

    @pl.when(step == 0)
    def _():
        qb_scr[...] = _query_columns(q_ref[...])
        m_scr[...] = jnp.broadcast_to(_self_scores(q_ref[...], ks_ref[...], n_rows), (n_rows, LANES))
        l_scr[...] = jnp.ones_like(l_scr)
        for h in range(H_DIFF):
            acc_scr[h] = jnp.broadcast_to(vs_ref[:, h * DV_DIFF:(h + 1) * DV_DIFF], (n_rows, DV_DIFF))

    qb = qb_scr[...]
    s = jnp.concatenate([_page_scores(k_refs[pg][...], qb, n_rows) for pg in range(n_pg)], axis=1)
    kpos = step * (n_pg * page) + lax.broadcasted_iota(jnp.int32, (1, n_pg * page), 1)
    s = s - slope_col * (past - kpos).astype(F32)
    m_old = m_scr[...]
    m_new = jnp.maximum(m_old, jnp.max(s, axis=-1, keepdims=True))
    alpha = jnp.exp(m_old - m_new)
    pe = jnp.exp(s - jnp.concatenate([m_new] * n_pg, axis=1))
    l_scr[...] = alpha * l_scr[...] + jnp.sum(pe, axis=-1, keepdims=True)
    pe = pe.astype(BF16)
    for h in range(H_DIFF):
        pv = None
        for pg in range(n_pg):
            vh = v_refs[pg][pl.ds(h, page, stride=H_DIFF), :].astype(BF16)
            d = _dot(pe[:, pg * page:(pg + 1) * page], vh)
            pv = d if pv is None else pv + d
        acc_scr[h] = alpha * acc_scr[h] + pv
    m_scr[...] = m_new

    @pl.when(step == pl.num_programs(1) - 1)
    def _():
        lam = _lambda(lamp_ref[...], lam_init)
        row = lax.broadcasted_iota(jnp.int32, (n_rows, DV_DIFF), 0)
        inv_l = 1.0 / l_scr[...]
        parts = []
        for h in range(H_DIFF):
            coef = jnp.where(row == 2 * h, 1.0, jnp.where(row == 2 * h + 1, -lam, 0.0))
            o = jnp.sum(acc_scr[h] * inv_l * coef, axis=0, keepdims=True)
            parts.append(_rms(o, g_ref[...]))
        o_ref[...] = jnp.concatenate(parts, axis=1) * (1.0 - lam_init)


def _diff_decode(page_table, q, k_self, v_self, lamp, g_subln, cache_kt, cache_v, *,
                 layer_off, n_pg, lam_init):
    nb, n_pages = page_table.shape
    page = cache_kt.shape[2]
    past = n_pages * page
    assert n_pages % n_pg == 0 and page == LANES

    def page_spec(pg, rows, cols):
        return pl.BlockSpec((None, rows, cols),
                            lambda b, s, pt: (layer_off + pt[b * n_pages + s * n_pg + pg], 0, 0))

    vec_spec = pl.BlockSpec((None, 1, GROUP), lambda b, s, pt: (b, 0, 0))
    n_rows = 2 * H_DIFF
    grid_spec = pltpu.PrefetchScalarGridSpec(
        num_scalar_prefetch=1,
        grid=(nb, n_pages // n_pg),
        in_specs=[vec_spec, vec_spec, vec_spec,
                  pl.BlockSpec((4, DH_DIFF), lambda b, s, pt: (0, 0)),
                  pl.BlockSpec((1, DV_DIFF), lambda b, s, pt: (0, 0))]
                 + [page_spec(pg, GROUP, page) for pg in range(n_pg)]
                 + [page_spec(pg, page * H_DIFF, DV_DIFF) for pg in range(n_pg)],
        out_specs=vec_spec,
        scratch_shapes=[pltpu.VMEM((GROUP, LANES), F32), pltpu.VMEM((n_rows, LANES), F32),
                        pltpu.VMEM((n_rows, LANES), F32), pltpu.VMEM((H_DIFF, n_rows, DV_DIFF), F32)],
    )
    out = pl.pallas_call(
        functools.partial(_diff_decode_kernel, n_pg=n_pg, past=past, lam_init=lam_init),
        grid_spec=grid_spec,
        out_shape=jax.ShapeDtypeStruct((nb, 1, GROUP), F32),
        compiler_params=_params(("arbitrary", "arbitrary")),
    )(page_table.reshape(-1), q.reshape(nb, 1, GROUP), k_self.reshape(nb, 1, GROUP),
      v_self.reshape(nb, 1, GROUP), lamp, g_subln.reshape(1, DV_DIFF),
      *([cache_kt] * n_pg), *([cache_v] * n_pg))
    return out.reshape(nb, GROUP)


def _moba_scores_kernel(pt_ref, q_ref, ks_ref, *rest, n_pg, n_pages, past):
    k_refs = rest[:n_pg]
    p_ref, sel_ref, pself_ref, qb_scr, s_scr = rest[n_pg:]
    step = pl.program_id(1)
    page = k_refs[0].shape[1]

    @pl.when(step == 0)
    def _():
        qb_scr[...] = _query_columns(q_ref[...])

    qb = qb_scr[...]
    for pg in range(n_pg):
        s_scr[step * n_pg + pg] = _page_scores(k_refs[pg][...], qb, H_MOBA)

    @pl.when(step == pl.num_programs(1) - 1)
    def _():
        slopes = _alibi_slopes(H_MOBA)
        s_self = _self_scores(q_ref[...], ks_ref[...], H_MOBA)
        pg_i = lax.broadcasted_iota(jnp.int32, (n_pages, page), 0)
        pg_f = pg_i.astype(F32)
        blk_f = jnp.floor(pg_f * 0.5)
        dist = (past - (pg_i * page + lax.broadcasted_iota(jnp.int32, (n_pages, page), 1))).astype(F32)
        row = lax.broadcasted_iota(jnp.int32, (H_MOBA, LANES), 0)
        lane = lax.broadcasted_iota(jnp.int32, (H_MOBA, LANES), 1)
        sel_out = jnp.zeros((H_MOBA, LANES), F32)
        pself_out = jnp.zeros((H_MOBA, LANES), F32)
        for h in range(H_MOBA):
            sh = s_scr[:, h, :]
            rs = jnp.broadcast_to(jnp.sum(sh, axis=-1, keepdims=True), (n_pages, page))
            other = jnp.where(pg_i % 2 == 0, pltpu.roll(rs, n_pages - 1, 0), pltpu.roll(rs, 1, 0))
            g = (rs + other) * (1.0 / MOBA_BLOCK)
            chosen = jnp.zeros((n_pages, page), jnp.bool_)
            for r in range(MOBA_TOPK):
                mx = jnp.max(g, axis=0, keepdims=True)
                first = jnp.min(jnp.where(g == mx, pg_f, float(n_pages)), axis=0, keepdims=True)
                idx = jnp.floor(first * 0.5)
                pick = blk_f == idx
                chosen = chosen | pick
                g = jnp.where(pick, REMOVED, g)
                sel_out = jnp.where((row == h) & (lane == r), idx, sel_out)
            logit = jnp.where(chosen, sh - float(slopes[h]) * dist, NEG_INF)
            ss = s_self[h:h + 1]
            m = jnp.maximum(jnp.max(jnp.max(logit, axis=-1, keepdims=True), axis=0, keepdims=True), ss)
            pe = jnp.exp(logit - m)
            ps = jnp.exp(ss - m)
            inv = 1.0 / (jnp.sum(jnp.sum(pe, axis=-1, keepdims=True), axis=0, keepdims=True) + ps)
            p_ref[h] = pe * inv
            pself_out = jnp.where(row == h, ps * inv, pself_out)
        sel_ref[...] = sel_out.astype(jnp.int32)
        pself_ref[...] = pself_out


def _moba_scores(page_table, q, k_self, cache_kt, *, layer_off, n_pg):
    nb, n_pages = page_table.shape
    page = cache_kt.shape[2]
    past = n_pages * page
    n_blk = past // MOBA_BLOCK
    pg_per_blk = MOBA_BLOCK // page
    assert n_blk >= MOBA_TOPK and n_pages % n_pg == 0 and pg_per_blk == 2 and page == LANES

    def page_spec(pg):
        return pl.BlockSpec((None, GROUP, page),
                            lambda b, s, pt: (layer_off + pt[b * n_pages + s * n_pg + pg], 0, 0))

    vec_spec = pl.BlockSpec((None, 1, GROUP), lambda b, s, pt: (b, 0, 0))
    small_spec = pl.BlockSpec((None, H_MOBA, LANES), lambda b, s, pt: (b, 0, 0))
    grid_spec = pltpu.PrefetchScalarGridSpec(
        num_scalar_prefetch=1,
        grid=(nb, n_pages // n_pg),
        in_specs=[vec_spec, vec_spec] + [page_spec(pg) for pg in range(n_pg)],
        out_specs=[pl.BlockSpec((None, H_MOBA, n_pages, page), lambda b, s, pt: (b, 0, 0, 0)),
                   small_spec, small_spec],
        scratch_shapes=[pltpu.VMEM((GROUP, LANES), F32), pltpu.VMEM((n_pages, H_MOBA, page), F32)],
    )
    return pl.pallas_call(
        functools.partial(_moba_scores_kernel, n_pg=n_pg, n_pages=n_pages, past=past),
        grid_spec=grid_spec,
        out_shape=[jax.ShapeDtypeStruct((nb, H_MOBA, n_pages, page), F32),
                   jax.ShapeDtypeStruct((nb, H_MOBA, LANES), jnp.int32),
                   jax.ShapeDtypeStruct((nb, H_MOBA, LANES), F32)],
        compiler_params=_params(("arbitrary", "arbitrary")),
    )(page_table.reshape(-1), q.reshape(nb, 1, GROUP), k_self.reshape(nb, 1, GROUP),
      *([cache_kt] * n_pg))


def _moba_values_kernel(pt_ref, sel_ref, p_ref, pself_ref, vs_ref, *rest, n_chunk):
    v_refs = rest[:n_chunk]
    o_ref = rest[n_chunk]
    b = pl.program_id(0)
    pr = pl.program_id(1)
    page = v_refs[0].shape[1]
    pg_per_blk = MOBA_BLOCK // page
    lane = lax.broadcasted_iota(jnp.int32, (SUBLANES, LANES), 1)
    pair = jnp.zeros((SUBLANES, LANES), F32)
    for hf in range(2):
        h = 2 * pr + hf
        acc = pself_ref[pl.ds(h, 1), :] * vs_ref[...]
        acc = jnp.broadcast_to(acc, (SUBLANES, LANES))
        for r in range(MOBA_TOPK):
            blk = sel_ref[(b * H_MOBA + h) * MOBA_TOPK + r]
            for pg in range(pg_per_blk):
                pp = jnp.broadcast_to(p_ref[hf, pl.ds(blk * pg_per_blk + pg, 1), :], (SUBLANES, page))
                vt = v_refs[(hf * MOBA_TOPK + r) * pg_per_blk + pg][...]
                acc = acc + _dot_nt(pp.astype(BF16), vt.astype(BF16))
        pair = jnp.where(lane // HALF == hf, acc, pair)
    o_ref[...] = pair[0:1]


def _moba_values(page_table, sel, p, pself, v_self, cache_vt, *, layer_off):
    nb, n_pages = page_table.shape
    page = cache_vt.shape[2]
    pg_per_blk = MOBA_BLOCK // page
    n_pair = H_MOBA // 2
    n_chunk = 2 * MOBA_TOPK * pg_per_blk

    def chunk_spec(hf, r, pg):
        def index_map(b, pr, pt, sl):
            blk = sl[(b * H_MOBA + 2 * pr + hf) * MOBA_TOPK + r]
            return (layer_off + pt[b * n_pages + blk * pg_per_blk + pg], pr, 0)
        return pl.BlockSpec((None, LANES, page), index_map)

    pair_vec = pl.BlockSpec((None, 1, LANES), lambda b, pr, pt, sl: (b, 0, pr))
    grid_spec = pltpu.PrefetchScalarGridSpec(
        num_scalar_prefetch=2,
        grid=(nb, n_pair),
        in_specs=[pl.BlockSpec((None, 2, n_pages, page), lambda b, pr, pt, sl: (b, pr, 0, 0)),
                  pl.BlockSpec((None, H_MOBA, LANES), lambda b, pr, pt, sl: (b, 0, 0)),
                  pair_vec]
                 + [chunk_spec(hf, r, pg) for hf in range(2) for r in range(MOBA_TOPK)
                    for pg in range(pg_per_blk)],
        out_specs=pair_vec,
    )
    out = pl.pallas_call(
        functools.partial(_moba_values_kernel, n_chunk=n_chunk),
        grid_spec=grid_spec,
        out_shape=jax.ShapeDtypeStruct((nb, 1, GROUP), F32),
        compiler_params=_params(("arbitrary", "arbitrary")),
    )(page_table.reshape(-1), sel.reshape(-1), p, pself, v_self.reshape(nb, 1, GROUP),
      *([cache_vt] * n_chunk))
    return out.reshape(nb, GROUP)


PROMPT_TILE = 512
ATTN_TILE = 512
DECODE_PAGES = 8


def _feature_major(cache, n_lead):
    nd = cache.ndim
    perm = (0, 1) + tuple(range(3, nd)) + (2,)
    t = jnp.transpose(cache, perm)
    return t.reshape(n_lead, -1, cache.shape[2])


def kernel(x_prompt, x_sample, cache_diff_k, cache_diff_v, cache_moba_k, cache_moba_v, cache_mem_k, cache_mem_v, page_table, mem_prompt, g_pre_mix, w_in, lambda_q1, lambda_k1, lambda_q2, lambda_k2, g_subln, w_out, g_post_mix, g_mem, w_mem_k, w_mem_v, g_pre_x, w_xq, w_xo, g_post_x, g_pre_ff, w_up, w_down, g_post_ff):
    nb_p, seq, d = x_prompt.shape
    nb_s, s_len, _ = x_sample.shape
    assert s_len == 1, "the sample kernels handle one new token per sequence"
    depth = w_in.shape[0]
    n_pool, page = cache_diff_k.shape[1], cache_diff_k.shape[2]
    assert MOBA_BLOCK % page == 0 and (page_table.shape[1] * page) % MOBA_BLOCK == 0
    n_blk_seq = seq // MOBA_BLOCK
    tm = min(PROMPT_TILE, seq)
    tq = min(ATTN_TILE, seq)

    xp = x_prompt.reshape(nb_p * seq, d)
    xs = x_sample.reshape(nb_s, d)
    ckt_d = _feature_major(cache_diff_k, depth * n_pool)
    cv_d = cache_diff_v.reshape(depth * n_pool, page * H_DIFF, DV_DIFF)
    ckt_m = _feature_major(cache_moba_k, depth * n_pool)
    cvt_m = _feature_major(cache_moba_v, depth * n_pool)
    mem_x = mem_prompt.reshape(nb_p * N_MEM, d)
    slopes_d = _alibi_slopes(H_DIFF)
    slopes_m = _alibi_slopes(H_MOBA)
    ktab = _key_aug_table(seq)
    aug_d = jnp.asarray(_query_aug_table(np.stack([slopes_d, slopes_d], axis=1)))
    aug_m = jnp.asarray(_query_aug_table(slopes_m.reshape(-1, 2)).reshape(1, GROUP))

    outs = [[] for _ in range(10)]
    for l in range(depth):
        lam_init = 0.8 - 0.6 * math.exp(-0.3 * l)
        lamp = jnp.stack([lambda_q1[l], lambda_k1[l], lambda_q2[l], lambda_k2[l]]).astype(F32)
        w_in_bf = w_in[l].astype(BF16)
        w_out_bf = w_out[l].astype(BF16)
        wts = {
            'w_out_a': w_out_bf[:GROUP], 'w_out_b': w_out_bf[GROUP:], 'g_post_mix': g_post_mix[l],
            'g_pre_x': g_pre_x[l], 'w_xq': w_xq[l].astype(BF16), 'w_xo': w_xo[l].astype(BF16),
            'g_post_x': g_post_x[l], 'g_pre_ff': g_pre_ff[l], 'w_up': w_up[l].astype(BF16),
            'w_down': w_down[l].astype(BF16), 'g_post_ff': g_post_ff[l],
        }

        (dq, dk, dk_bf, dv, dv_bf, mq, mk, mk_bf, mv, mv_bf, mq_aug) = _project_prompt(
            xp, g_pre_mix[l], w_in_bf, aug_m, tm=tm, n_blk_seq=n_blk_seq)
        o_d = _pair_attention(dq, aug_d, dk_bf, ktab, dv_bf, nb=nb_p, seq=seq, tq=tq, moba=False,
                              lamp=lamp, g_subln=g_subln[l], lam_init=lam_init)
        o_m = _pair_attention(mq, mq_aug, mk_bf, ktab, mv_bf, nb=nb_p, seq=seq, tq=tq, moba=True)
        mem_tm = min(PROMPT_TILE, nb_p * N_MEM)
        memk = _norm_matmul(mem_x, g_mem[l], w_mem_k[l].astype(BF16), tm=mem_tm, out_dtype=F32)
        memv = _norm_matmul(mem_x, g_mem[l], w_mem_v[l].astype(BF16), tm=mem_tm, out_dtype=F32)
        xp = _finish_layer(xp, o_d, o_m, memk.reshape(nb_p, N_MEM, -1), memv.reshape(nb_p, N_MEM, -1),
                           wts, nb=nb_p, rows=seq, tm=tm)
        outs[0].append(dk.reshape(nb_p, seq, H_DIFF, 2, DH_DIFF))
        outs[1].append(dv.reshape(nb_p, seq, H_DIFF, DV_DIFF))
        outs[2].append(mk.reshape(nb_p, seq, H_MOBA, DH_MOBA))
        outs[3].append(mv.reshape(nb_p, seq, H_MOBA, DH_MOBA))
        outs[4].append(memk.reshape(nb_p, N_MEM, H_MEM, DH_MEM))
        outs[5].append(memv.reshape(nb_p, N_MEM, H_MEM, DH_MEM))

        sq, sk, sv, tq_s, tk, tv = _project_sample(xs, g_pre_mix[l], w_in_bf)
        o_ds = _diff_decode(page_table, sq, sk, sv, lamp, g_subln[l], ckt_d, cv_d,
                            layer_off=l * n_pool, n_pg=DECODE_PAGES, lam_init=lam_init)
        p_s, sel_s, pself_s = _moba_scores(page_table, tq_s, tk, ckt_m, layer_off=l * n_pool,
                                           n_pg=DECODE_PAGES)
        o_ms = _moba_values(page_table, sel_s[:, :, :MOBA_TOPK], p_s, pself_s, tv, cvt_m,
                            layer_off=l * n_pool)
        xs = _finish_layer(xs, o_ds, o_ms, cache_mem_k[l].reshape(nb_s, N_MEM, -1),
                           cache_mem_v[l].reshape(nb_s, N_MEM, -1), wts, nb=nb_s, rows=1, tm=nb_s,
                           rows_pad=SUBLANES)
        outs[6].append(sk.reshape(nb_s, 1, H_DIFF, 2, DH_DIFF))
        outs[7].append(sv.reshape(nb_s, 1, H_DIFF, DV_DIFF))
        outs[8].append(tk.reshape(nb_s, 1, H_MOBA, DH_MOBA))
        outs[9].append(tv.reshape(nb_s, 1, H_MOBA, DH_MOBA))

    return (xp.reshape(nb_p, seq, d), xs.reshape(nb_s, 1, d), *[jnp.stack(o) for o in outs])
```

```python
import functools
import math

import numpy as np
import jax
import jax.numpy as jnp
from jax import lax
from jax.experimental import pallas as pl
from jax.experimental.pallas import tpu as pltpu

F32 = jnp.float32
BF16 = jnp.bfloat16

EPS = 1e-6
NEG_INF = -1e30
REMOVED = -3e38
LOG2E = 1.4426950408889634

H_DIFF = 4
DH_DIFF = 64
DV_DIFF = 2 * DH_DIFF
H_MOBA = 8
DH_MOBA = 64
MOBA_BLOCK = 256
MOBA_TOPK = 3
H_MEM = 4
DH_MEM = 128
N_MEM = 256
GROUP = 512
LANES = 128
SUBLANES = 8
HALF = LANES // 2
SEL_STRIDE = LANES // H_MOBA

AUG_SEL = 0
AUG_POS_IN = 16
AUG_POS_BLK = 19
N_SPLIT = 3

VMEM_LIMIT_BYTES = 56 * 1024 * 1024


def _alibi_slopes(n):
    return np.power(2.0, -8.0 * np.arange(1, n + 1) / n).astype(np.float32)


def _split_bf16(x):
    parts, rest = [], np.asarray(x, np.float32)
    for _ in range(N_SPLIT):
        p = rest.astype(BF16).astype(np.float32)
        parts.append(p)
        rest = rest - p
    return parts


def _query_aug_table(slopes_per_half):
    slopes_per_half = np.asarray(slopes_per_half, np.float32)
    tab = np.zeros((slopes_per_half.shape[0], 1, LANES), np.float32)
    for hf in range(2):
        for i, part in enumerate(_split_bf16(slopes_per_half[:, hf] * np.float32(LOG2E))):
            tab[:, 0, hf * HALF + AUG_POS_IN + i] = part
            tab[:, 0, hf * HALF + AUG_POS_BLK + i] = part
    return tab


def _key_aug_table(seq):
    t = np.arange(seq)
    tab = np.zeros((seq, LANES), np.float32)
    for hf in range(2):
        tab[t, hf * HALF + AUG_SEL + t // MOBA_BLOCK] = 1.0
        tab[:, hf * HALF + AUG_POS_IN:hf * HALF + AUG_POS_IN + N_SPLIT] = (t % MOBA_BLOCK)[:, None]
        tab[:, hf * HALF + AUG_POS_BLK:hf * HALF + AUG_POS_BLK + N_SPLIT] = (t - t % MOBA_BLOCK)[:, None]
    return jnp.asarray(tab, BF16)


def _params(sem):
    return pltpu.CompilerParams(dimension_semantics=sem, vmem_limit_bytes=VMEM_LIMIT_BYTES)


def _rms(x, g):
    return x * lax.rsqrt(jnp.mean(x * x, axis=-1, keepdims=True) + EPS) * g


def _dot(a, b):
    return jnp.dot(a, b, preferred_element_type=F32)


def _dot_nt(a, b, precision=None):
    return lax.dot_general(a, b, (((1,), (1,)), ((), ())), preferred_element_type=F32,
                           precision=precision)


def _lambda(lp, lam_init):
    return (jnp.exp(jnp.sum(lp[0:1] * lp[1:2], axis=-1, keepdims=True))
            - jnp.exp(jnp.sum(lp[2:3] * lp[3:4], axis=-1, keepdims=True)) + lam_init)


def _proj_prompt_kernel(x_ref, g_ref, w_ref, atab_ref, dq_ref, dkt_ref, dkb_ref, dv_ref, dvb_ref,
                        mq_ref, mkt_ref, mkb_ref, mvt_ref, mvb_ref, aug_ref, kmt_ref, *, tm, n_blk_seq):
    xn = _rms(x_ref[...], g_ref[...]).astype(BF16)

    def col(c):
        return _dot(xn, w_ref[:, c * GROUP:(c + 1) * GROUP])

    dq_ref[...] = (col(0) * (DH_DIFF ** -0.5 * LOG2E)).astype(BF16)
    z = col(1)
    dkt_ref[...] = z.T
    dkb_ref[...] = z.astype(BF16)
    z = col(2)
    dv_ref[...] = z
    dvb_ref[...] = z.astype(BF16)
    mq = col(3)
    mq_ref[...] = (mq * (DH_MOBA ** -0.5 * LOG2E)).astype(BF16)
    mk = col(4)
    mkt_ref[...] = mk.T
    mkb_ref[...] = mk.astype(BF16)
    z = col(5)
    mvt_ref[...] = z.T
    mvb_ref[...] = z.astype(BF16)

    t = pl.program_id(0)
    blk_per_tile = tm // MOBA_BLOCK
    tiles_per_seq = n_blk_seq // blk_per_tile
    blk0 = (t % tiles_per_seq) * blk_per_tile

    @pl.when(t == 0)
    def _():
        kmt_ref[...] = jnp.zeros_like(kmt_ref)

    lane_g = lax.broadcasted_iota(jnp.int32, (1, GROUP), 1) // DH_MOBA
    for r in range(blk_per_tile):
        km = jnp.mean(mk[r * MOBA_BLOCK:(r + 1) * MOBA_BLOCK], axis=0, keepdims=True)
        for h in range(H_MOBA):
            kmt_ref[pl.ds(h * SEL_STRIDE + blk0 + r, 1), :] = jnp.where(lane_g == h, km, 0.0)

    gate = _dot_nt(mq * (DH_MOBA ** -0.5), kmt_ref[...], precision=lax.Precision.HIGHEST)
    lane = lax.broadcasted_iota(jnp.int32, (tm, LANES), 1)
    row = lax.broadcasted_iota(jnp.int32, (tm, LANES), 0)
    n_in = lane % SEL_STRIDE
    own = blk0 + row // MOBA_BLOCK
    past = n_in < own
    g = jnp.where(past, gate, NEG_INF)
    cnt = jnp.zeros((tm, LANES), jnp.int32)
    for s in range(1, n_blk_seq):
        lo = pltpu.roll(g, s, 1)
        cnt += jnp.where((n_in >= s) & (lo >= g), 1, 0)
        hi = pltpu.roll(g, LANES - s, 1)
        cnt += jnp.where((n_in + s < SEL_STRIDE) & (hi > g), 1, 0)
    visible = (past & (cnt < MOBA_TOPK)) | (n_in == own)
    sel = jnp.where(visible, 0.0, NEG_INF).astype(BF16)
    er = lax.broadcasted_iota(jnp.int32, (LANES, GROUP), 0)
    ec = lax.broadcasted_iota(jnp.int32, (LANES, GROUP), 1)
    spread = jnp.where((ec // HALF == er // SEL_STRIDE) & (ec % HALF == AUG_SEL + er % SEL_STRIDE),
                       1.0, 0.0).astype(BF16)
    aug_ref[...] = (_dot(sel, spread) + atab_ref[...]).astype(BF16)


def _project_prompt(x2d, g, w_bf, aug_tab, *, tm, n_blk_seq):
    n, d = x2d.shape
    assert tm % MOBA_BLOCK == 0 and n_blk_seq % (tm // MOBA_BLOCK) == 0 and n_blk_seq <= SEL_STRIDE
    seq = n_blk_seq * MOBA_BLOCK
    tiles_per_seq = seq // tm
    row_spec = pl.BlockSpec((tm, GROUP), lambda t: (t, 0))
    t_spec = pl.BlockSpec((None, GROUP, tm), lambda t: (t // tiles_per_seq, 0, t % tiles_per_seq))
    f32_out = jax.ShapeDtypeStruct((n, GROUP), F32)
    bf_out = jax.ShapeDtypeStruct((n, GROUP), BF16)
    t_out = jax.ShapeDtypeStruct((n // seq, GROUP, seq), F32)
    return pl.pallas_call(
        functools.partial(_proj_prompt_kernel, tm=tm, n_blk_seq=n_blk_seq),
        grid=(n // tm,),
        in_specs=[pl.BlockSpec((tm, d), lambda t: (t, 0)),
                  pl.BlockSpec((1, d), lambda t: (0, 0)),
                  pl.BlockSpec((d, 6 * GROUP), lambda t: (0, 0)),
                  pl.BlockSpec((1, GROUP), lambda t: (0, 0))],
        out_specs=[row_spec, t_spec, row_spec, row_spec, row_spec, row_spec, t_spec, row_spec, t_spec,
                   row_spec, row_spec],
        out_shape=[bf_out, t_out, bf_out, f32_out, bf_out, bf_out, t_out, bf_out, t_out, bf_out, bf_out],
        scratch_shapes=[pltpu.VMEM((LANES, GROUP), F32)],
        compiler_params=_params(("arbitrary",)),
    )(x2d, g.reshape(1, d), w_bf, aug_tab)


def _proj_sample_kernel(x_ref, g_ref, w_ref, dq_ref, dk_ref, dv_ref, mq_ref, mk_ref, mv_ref):
    xn = _rms(x_ref[...], g_ref[...])
    scales = (DH_DIFF ** -0.5, 1.0, 1.0, DH_MOBA ** -0.5, 1.0, 1.0)
    for c, (ref, sc) in enumerate(zip((dq_ref, dk_ref, dv_ref, mq_ref, mk_ref, mv_ref), scales)):
        ref[...] = jnp.dot(xn, w_ref[:, c * GROUP:(c + 1) * GROUP], preferred_element_type=F32,
                           precision=lax.Precision.HIGHEST) * sc


def _project_sample(x2d, g, w_f32):
    n, d = x2d.shape
    out = jax.ShapeDtypeStruct((n, GROUP), F32)
    return pl.pallas_call(
        _proj_sample_kernel,
        grid=(1,),
        in_specs=[pl.BlockSpec((n, d), lambda t: (0, 0)),
                  pl.BlockSpec((1, d), lambda t: (0, 0)),
                  pl.BlockSpec((d, 6 * GROUP), lambda t: (0, 0))],
        out_specs=[pl.BlockSpec((n, GROUP), lambda t: (0, 0))] * 6,
        out_shape=[out] * 6,
        compiler_params=_params(("arbitrary",)),
    )(x2d, g.reshape(1, d), w_f32)


def _pair_attn_kernel(q_ref, aug_ref, k_ref, ktab_ref, v_ref, *rest, tq, moba, lam_init):
    if moba:
        o_ref, m_scr, acc_scr, s0_scr, s1_scr = rest
    else:
        lamp_ref, g_ref, o_ref, m_scr, acc_scr, s0_scr, s1_scr = rest
    i = pl.program_id(2)

    q = q_ref[...]
    if moba:
        aug = aug_ref[...]
    else:
        aug = jnp.broadcast_to(aug_ref[...], (tq, LANES)).astype(BF16)
    lane = lax.broadcasted_iota(jnp.int32, (tq, LANES), 1)
    lo = lane < HALF
    zero = jnp.zeros_like(q)
    q2 = jnp.concatenate(
        [jnp.concatenate([jnp.where(lo, q, zero), jnp.where(lo, aug, zero)], axis=1),
         jnp.concatenate([jnp.where(lo, zero, q), jnp.where(lo, zero, aug)], axis=1)], axis=0)

    m_scr[...] = jnp.full_like(m_scr, NEG_INF)
    acc_scr[...] = jnp.zeros_like(acc_scr)
    ones = jnp.ones((tq, LANES), BF16)
    n_rep = tq // LANES

    def scores(jb):
        start = pl.multiple_of(jb * tq, tq)
        kj = jnp.concatenate([k_ref[pl.ds(start, tq), :], ktab_ref[pl.ds(start, tq), :]], axis=1)
        return _dot_nt(q2, kj)

    def absorb(s, jb):
        start = pl.multiple_of(jb * tq, tq)
        vj = jnp.concatenate([v_ref[pl.ds(start, tq), :], ones], axis=1)
        m_old = m_scr[...]
        m_new = jnp.maximum(m_old, jnp.max(s, axis=-1, keepdims=True))
        alpha = jnp.exp2(m_old - m_new)
        pe = jnp.exp2(s - jnp.concatenate([m_new] * n_rep, axis=1))
        acc_scr[...] = jnp.concatenate([alpha, alpha], axis=1) * acc_scr[...] + _dot(pe.astype(BF16), vj)
        m_scr[...] = m_new

    r = lax.broadcasted_iota(jnp.int32, (2 * tq, tq), 0)
    c = lax.broadcasted_iota(jnp.int32, (2 * tq, tq), 1)
    s0_scr[...] = scores(0)
    absorb(jnp.where(r % tq >= c, scores(i), NEG_INF), i)
    last = jnp.maximum(i - 1, 0)

    def body(u, carry):
        s1_scr[...] = scores(jnp.minimum(2 * u + 1, last))
        absorb(s0_scr[...], 2 * u)

        @pl.when(2 * u + 1 < i)
        def _():
            s0_scr[...] = scores(jnp.minimum(2 * u + 2, last))
            absorb(s1_scr[...], 2 * u + 1)

        return carry

    lax.fori_loop(0, (i + 1) // 2, body, 0)

    acc = acc_scr[...]
    o2 = acc[:, :LANES] / acc[:, LANES:]
    o0, o1 = o2[:tq], o2[tq:]
    if moba:
        o_ref[...] = jnp.where(lo, o0, o1).astype(o_ref.dtype)
    else:
        o = o0 - _lambda(lamp_ref[...], lam_init) * o1
        o_ref[...] = (_rms(o, g_ref[...]) * (1.0 - lam_init)).astype(o_ref.dtype)


def _pair_attention(q_bf, aug, k_bf, ktab, v_bf, *, nb, seq, tq, moba, lamp=None, g_subln=None,
                    lam_init=0.0):
    n_pair = GROUP // LANES
    nq = seq // tq
    assert tq % MOBA_BLOCK == 0 and seq % tq == 0
    q_spec = pl.BlockSpec((tq, LANES), lambda b, p, i: (b * nq + i, p))
    kv_spec = pl.BlockSpec((seq, LANES), lambda b, p, i: (b, p))
    if moba:
        aug_spec = q_spec
    else:
        aug_spec = pl.BlockSpec((None, 1, LANES), lambda b, p, i: (p, 0, 0))
    in_specs = [q_spec, aug_spec, kv_spec, pl.BlockSpec((seq, LANES), lambda b, p, i: (0, 0)), kv_spec]
    args = [q_bf, aug, k_bf, ktab, v_bf]
    if not moba:
        in_specs += [pl.BlockSpec((4, DH_DIFF), lambda b, p, i: (0, 0)),
                     pl.BlockSpec((1, DV_DIFF), lambda b, p, i: (0, 0))]
        args += [lamp, g_subln.reshape(1, DV_DIFF)]
    return pl.pallas_call(
        functools.partial(_pair_attn_kernel, tq=tq, moba=moba, lam_init=lam_init),
        grid=(nb, n_pair, nq),
        in_specs=in_specs,
        out_specs=q_spec,
        out_shape=jax.ShapeDtypeStruct((nb * seq, GROUP), BF16),
        scratch_shapes=[pltpu.VMEM((2 * tq, LANES), F32), pltpu.VMEM((2 * tq, 2 * LANES), F32),
                        pltpu.VMEM((2 * tq, tq), F32), pltpu.VMEM((2 * tq, tq), F32)],
        compiler_params=_params(("arbitrary", "arbitrary", "arbitrary")),
    )(*args)


def _norm_mm_kernel(x_ref, g_ref, w_ref, o_ref, *, scale, relu2):
    y = _dot(_rms(x_ref[...], g_ref[...]).astype(BF16), w_ref[...])
    if relu2:
        y = jnp.square(jnp.maximum(y, 0.0))
    if scale != 1.0:
        y = y * scale
    o_ref[...] = y.astype(o_ref.dtype)


def _norm_matmul(x2d, g, w_bf, *, tm, out_dtype, scale=1.0, relu2=False):
    n, d = x2d.shape
    e = w_bf.shape[1]
    return pl.pallas_call(
        functools.partial(_norm_mm_kernel, scale=scale, relu2=relu2),
        grid=(n // tm,),
        in_specs=[pl.BlockSpec((tm, d), lambda t: (t, 0)),
                  pl.BlockSpec((1, d), lambda t: (0, 0)),
                  pl.BlockSpec((d, e), lambda t: (0, 0))],
        out_specs=pl.BlockSpec((tm, e), lambda t: (t, 0)),
        out_shape=jax.ShapeDtypeStruct((n, e), out_dtype),
        compiler_params=_params(("arbitrary",)),
    )(x2d, g.reshape(1, d), w_bf)


def _mm_norm_res_kernel(*refs, n_in):
    a_refs = refs[:n_in]
    w_refs = refs[n_in:2 * n_in]
    x_ref, g_ref, o_ref = refs[2 * n_in:]
    y = _dot(a_refs[0][...].astype(BF16), w_refs[0][...])
    for a_ref, w_ref in zip(a_refs[1:], w_refs[1:]):
        y = y + _dot(a_ref[...].astype(BF16), w_ref[...])
    o_ref[...] = x_ref[...] + _rms(y, g_ref[...])


def _matmul_norm_residual(a_list, w_list, x2d, g, *, tm):
    n, d = x2d.shape
    n_in = len(a_list)
    in_specs = [pl.BlockSpec((tm, a.shape[1]), lambda t: (t, 0)) for a in a_list]
    in_specs += [pl.BlockSpec(w.shape, lambda t: (0, 0)) for w in w_list]
    in_specs += [pl.BlockSpec((tm, d), lambda t: (t, 0)), pl.BlockSpec((1, d), lambda t: (0, 0))]
    return pl.pallas_call(
        functools.partial(_mm_norm_res_kernel, n_in=n_in),
        grid=(n // tm,),
        in_specs=in_specs,
        out_specs=pl.BlockSpec((tm, d), lambda t: (t, 0)),
        out_shape=jax.ShapeDtypeStruct((n, d), F32),
        compiler_params=_params(("arbitrary",)),
    )(*a_list, *w_list, x2d, g.reshape(1, d))


def _mem_attn_kernel(q_ref, k_ref, v_ref, o_ref):
    q = q_ref[...]
    outs = []
    for h in range(H_MEM):
        cols = slice(h * DH_MEM, (h + 1) * DH_MEM)
        s = _dot_nt(q[:, cols], k_ref[:, cols].astype(BF16))
        pe = jnp.exp(s - jnp.max(s, axis=-1, keepdims=True))
        o = _dot(pe.astype(BF16), v_ref[:, cols].astype(BF16))
        outs.append(o / jnp.sum(pe, axis=-1, keepdims=True))
    o_ref[...] = jnp.concatenate(outs, axis=1).astype(o_ref.dtype)


def _mem_attention(q_bf, mem_k, mem_v, *, nb, rows, tm):
    d = H_MEM * DH_MEM
    nt = rows // tm
    return pl.pallas_call(
        _mem_attn_kernel,
        grid=(nb, nt),
        in_specs=[pl.BlockSpec((tm, d), lambda b, t: (b * nt + t, 0)),
                  pl.BlockSpec((None, N_MEM, d), lambda b, t: (b, 0, 0)),
                  pl.BlockSpec((None, N_MEM, d), lambda b, t: (b, 0, 0))],
        out_specs=pl.BlockSpec((tm, d), lambda b, t: (b * nt + t, 0)),
        out_shape=jax.ShapeDtypeStruct((nb * rows, d), BF16),
        compiler_params=_params(("arbitrary", "arbitrary")),
    )(q_bf, mem_k, mem_v)


def _finish_layer(x2d, o_diff, o_moba, mem_k, mem_v, wts, *, nb, rows, tm, rows_pad=1):
    x1 = _matmul_norm_residual([o_diff, o_moba], [wts['w_out_a'], wts['w_out_b']], x2d,
                               wts['g_post_mix'], tm=tm)
    q = _norm_matmul(x1, wts['g_pre_x'], wts['w_xq'], tm=tm, out_dtype=BF16, scale=DH_MEM ** -0.5)
    if rows_pad > 1:
        qp = jnp.broadcast_to(q[:, None, :], (nb * rows, rows_pad, q.shape[1])).reshape(-1, q.shape[1])
        c = _mem_attention(qp, mem_k, mem_v, nb=nb, rows=rows * rows_pad, tm=rows * rows_pad)
        c = c.reshape(nb * rows, rows_pad, -1)[:, 0, :]
    else:
        c = _mem_attention(q, mem_k, mem_v, nb=nb, rows=rows, tm=min(tm, rows))
    x2 = _matmul_norm_residual([c], [wts['w_xo']], x1, wts['g_post_x'], tm=tm)
    hdn = _norm_matmul(x2, wts['g_pre_ff'], wts['w_up'], tm=tm, out_dtype=BF16, relu2=True)
    return _matmul_norm_residual([hdn], [wts['w_down']], x2, wts['g_post_ff'], tm=tm)


def _query_columns(q_row):
    r = lax.broadcasted_iota(jnp.int32, (GROUP, GROUP), 0)
    c = lax.broadcasted_iota(jnp.int32, (GROUP, GROUP), 1)
    col = jnp.sum(jnp.where(r == c, jnp.broadcast_to(q_row, (GROUP, GROUP)), 0.0), axis=-1, keepdims=True)
    return jnp.broadcast_to(col, (GROUP, LANES))


def _page_scores(kt, qb, n_grp):
    prod = kt * qb
    w = GROUP // n_grp
    return jnp.concatenate([jnp.sum(prod[g * w:(g + 1) * w], axis=0, keepdims=True) for g in range(n_grp)],
                           axis=0)


def _self_scores(q_row, k_row, n_grp):
    row = lax.broadcasted_iota(jnp.int32, (n_grp, GROUP), 0)
    lane = lax.broadcasted_iota(jnp.int32, (n_grp, GROUP), 1)
    prod = jnp.broadcast_to(q_row * k_row, (n_grp, GROUP))
    return jnp.sum(jnp.where(lane // (GROUP // n_grp) == row, prod, 0.0), axis=-1, keepdims=True)


def _diff_decode_kernel(pt_ref, q_ref, ks_ref, vs_ref, lamp_ref, g_ref, *rest, n_pg, past, lam_init):
    k_refs = rest[:n_pg]
    v_refs = rest[n_pg:2 * n_pg]
    o_ref, qb_scr, m_scr, l_scr, acc_scr = rest[2 * n_pg:]
    step = pl.program_id(1)
    n_rows = 2 * H_DIFF
    page = k_refs[0].shape[1]
    slopes = _alibi_slopes(H_DIFF)
    row1 = lax.broadcasted_iota(jnp.int32, (n_rows, 1), 0)
    slope_col = jnp.zeros((n_rows, 1), F32)
    for h in range(H_DIFF):
        slope_col = jnp.where(row1 // 2 == h, float(slopes[h]), slope_col)

    @pl.when(step == 0)
    def _():
        qb_scr[...] = _query_columns(q_ref[...])
        m_scr[...] = jnp.broadcast_to(_self_scores(q_ref[...], ks_ref[...], n_rows), (n_rows, LANES))
        l_scr[...] = jnp.ones_like(l_scr)
        for h in range(H_DIFF):
            acc_scr[h] = jnp.broadcast_to(vs_ref[:, h * DV_DIFF:(h + 1) * DV_DIFF], (n_rows, DV_DIFF))

    qb = qb_scr[...]
    s = jnp.concatenate([_page_scores(k_refs[pg][...], qb, n_rows) for pg in range(n_pg)], axis=1)
    kpos = step * (n_pg * page) + lax.broadcasted_iota(jnp.int32, (1, n_pg * page), 1)
    s = s - slope_col * (past - kpos).astype(F32)
    m_old = m_scr[...]
    m_new = jnp.maximum(m_old, jnp.max(s, axis=-1, keepdims=True))
    alpha = jnp.exp(m_old - m_new)
    pe = jnp.exp(s - jnp.concatenate([m_new] * n_pg, axis=1))
    l_scr[...] = alpha * l_scr[...] + jnp.sum(pe, axis=-1, keepdims=True)
    pe = pe.astype(BF16)
    for h in range(H_DIFF):
        pv = None
        for pg in range(n_pg):
            vh = v_refs[pg][pl.ds(h, page, stride=H_DIFF), :].astype(BF16)
            d = _dot(pe[:, pg * page:(pg + 1) * page], vh)
            pv = d if pv is None else pv + d
        acc_scr[h] = alpha * acc_scr[h] + pv
    m_scr[...] = m_new

    @pl.when(step == pl.num_programs(1) - 1)
    def _():
        lam = _lambda(lamp_ref[...], lam_init)
        row = lax.broadcasted_iota(jnp.int32, (n_rows, DV_DIFF), 0)
        inv_l = 1.0 / l_scr[...]
        parts = []
        for h in range(H_DIFF):
            coef = jnp.where(row == 2 * h, 1.0, jnp.where(row == 2 * h + 1, -lam, 0.0))
            o = jnp.sum(acc_scr[h] * inv_l * coef, axis=0, keepdims=True)
            parts.append(_rms(o, g_ref[...]))
        o_ref[...] = jnp.concatenate(parts, axis=1) * (1.0 - lam_init)


def _diff_decode(page_table, q, k_self, v_self, lamp, g_subln, cache_kt, cache_v, *,
                 layer_off, n_pg, lam_init):
    nb, n_pages = page_table.shape
    page = cache_kt.shape[2]
    past = n_pages * page
    assert n_pages % n_pg == 0 and page == LANES

    def page_spec(pg, rows, cols):
        return pl.BlockSpec((None, rows, cols),
                            lambda b, s, pt: (layer_off + pt[b * n_pages + s * n_pg + pg], 0, 0))

    vec_spec = pl.BlockSpec((None, 1, GROUP), lambda b, s, pt: (b, 0, 0))
    n_rows = 2 * H_DIFF
    grid_spec = pltpu.PrefetchScalarGridSpec(
        num_scalar_prefetch=1,
        grid=(nb, n_pages // n_pg),
        in_specs=[vec_spec, vec_spec, vec_spec,
                  pl.BlockSpec((4, DH_DIFF), lambda b, s, pt: (0, 0)),
                  pl.BlockSpec((1, DV_DIFF), lambda b, s, pt: (0, 0))]
                 + [page_spec(pg, GROUP, page) for pg in range(n_pg)]
                 + [page_spec(pg, page * H_DIFF, DV_DIFF) for pg in range(n_pg)],
        out_specs=vec_spec,
        scratch_shapes=[pltpu.VMEM((GROUP, LANES), F32), pltpu.VMEM((n_rows, LANES), F32),
                        pltpu.VMEM((n_rows, LANES), F32), pltpu.VMEM((H_DIFF, n_rows, DV_DIFF), F32)],
    )
    out = pl.pallas_call(
        functools.partial(_diff_decode_kernel, n_pg=n_pg, past=past, lam_init=lam_init),
        grid_spec=grid_spec,
        out_shape=jax.ShapeDtypeStruct((nb, 1, GROUP), F32),
        compiler_params=_params(("arbitrary", "arbitrary")),
    )(page_table.reshape(-1), q.reshape(nb, 1, GROUP), k_self.reshape(nb, 1, GROUP),
      v_self.reshape(nb, 1, GROUP), lamp, g_subln.reshape(1, DV_DIFF),
      *([cache_kt] * n_pg), *([cache_v] * n_pg))
    return out.reshape(nb, GROUP)


def _moba_scores_kernel(pt_ref, q_ref, ks_ref, *rest, n_pg, n_pages, past):
    k_refs = rest[:n_pg]
    p_ref, sel_ref, pself_ref, qb_scr, s_scr = rest[n_pg:]
    step = pl.program_id(1)
    page = k_refs[0].shape[1]

    @pl.when(step == 0)
    def _():
        qb_scr[...] = _query_columns(q_ref[...])

    qb = qb_scr[...]
    for pg in range(n_pg):
        s_scr[step * n_pg + pg] = _page_scores(k_refs[pg][...], qb, H_MOBA)

    @pl.when(step == pl.num_programs(1) - 1)
    def _():
        slopes = _alibi_slopes(H_MOBA)
        s_self = _self_scores(q_ref[...], ks_ref[...], H_MOBA)
        pg_i = lax.broadcasted_iota(jnp.int32, (n_pages, page), 0)
        pg_f = pg_i.astype(F32)
        blk_f = jnp.floor(pg_f * 0.5)
        dist = (past - (pg_i * page + lax.broadcasted_iota(jnp.int32, (n_pages, page), 1))).astype(F32)
        row = lax.broadcasted_iota(jnp.int32, (H_MOBA, LANES), 0)
        lane = lax.broadcasted_iota(jnp.int32, (H_MOBA, LANES), 1)
        sel_out = jnp.zeros((H_MOBA, LANES), F32)
        pself_out = jnp.zeros((H_MOBA, LANES), F32)
        for h in range(H_MOBA):
            sh = s_scr[:, h, :]
            rs = jnp.broadcast_to(jnp.sum(sh, axis=-1, keepdims=True), (n_pages, page))
            other = jnp.where(pg_i % 2 == 0, pltpu.roll(rs, n_pages - 1, 0), pltpu.roll(rs, 1, 0))
            g = (rs + other) * (1.0 / MOBA_BLOCK)
            chosen = jnp.zeros((n_pages, page), jnp.bool_)
            for r in range(MOBA_TOPK):
                mx = jnp.max(g, axis=0, keepdims=True)
                first = jnp.min(jnp.where(g == mx, pg_f, float(n_pages)), axis=0, keepdims=True)
                idx = jnp.floor(first * 0.5)
                pick = blk_f == idx
                chosen = chosen | pick
                g = jnp.where(pick, REMOVED, g)
                sel_out = jnp.where((row == h) & (lane == r), idx, sel_out)
            logit = jnp.where(chosen, sh - float(slopes[h]) * dist, NEG_INF)
            ss = s_self[h:h + 1]
            m = jnp.maximum(jnp.max(jnp.max(logit, axis=-1, keepdims=True), axis=0, keepdims=True), ss)
            pe = jnp.exp(logit - m)
            ps = jnp.exp(ss - m)
            inv = 1.0 / (jnp.sum(jnp.sum(pe, axis=-1, keepdims=True), axis=0, keepdims=True) + ps)
            p_ref[h] = pe * inv
            pself_out = jnp.where(row == h, ps * inv, pself_out)
        sel_ref[...] = sel_out.astype(jnp.int32)
        pself_ref[...] = pself_out


def _moba_scores(page_table, q, k_self, cache_kt, *, layer_off, n_pg):
    nb, n_pages = page_table.shape
    page = cache_kt.shape[2]
    past = n_pages * page
    n_blk = past // MOBA_BLOCK
    pg_per_blk = MOBA_BLOCK // page
    assert n_blk >= MOBA_TOPK and n_pages % n_pg == 0 and pg_per_blk == 2 and page == LANES

    def page_spec(pg):
        return pl.BlockSpec((None, GROUP, page),
                            lambda b, s, pt: (layer_off + pt[b * n_pages + s * n_pg + pg], 0, 0))

    vec_spec = pl.BlockSpec((None, 1, GROUP), lambda b, s, pt: (b, 0, 0))
    small_spec = pl.BlockSpec((None, H_MOBA, LANES), lambda b, s, pt: (b, 0, 0))
    grid_spec = pltpu.PrefetchScalarGridSpec(
        num_scalar_prefetch=1,
        grid=(nb, n_pages // n_pg),
        in_specs=[vec_spec, vec_spec] + [page_spec(pg) for pg in range(n_pg)],
        out_specs=[pl.BlockSpec((None, H_MOBA, n_pages, page), lambda b, s, pt: (b, 0, 0, 0)),
                   small_spec, small_spec],
        scratch_shapes=[pltpu.VMEM((GROUP, LANES), F32), pltpu.VMEM((n_pages, H_MOBA, page), F32)],
    )
    return pl.pallas_call(
        functools.partial(_moba_scores_kernel, n_pg=n_pg, n_pages=n_pages, past=past),
        grid_spec=grid_spec,
        out_shape=[jax.ShapeDtypeStruct((nb, H_MOBA, n_pages, page), F32),
                   jax.ShapeDtypeStruct((nb, H_MOBA, LANES), jnp.int32),
                   jax.ShapeDtypeStruct((nb, H_MOBA, LANES), F32)],
        compiler_params=_params(("arbitrary", "arbitrary")),
    )(page_table.reshape(-1), q.reshape(nb, 1, GROUP), k_self.reshape(nb, 1, GROUP),
      *([cache_kt] * n_pg))


def _moba_values_kernel(pt_ref, sel_ref, p_ref, pself_ref, vs_ref, *rest, n_chunk):
    v_refs = rest[:n_chunk]
    o_ref = rest[n_chunk]
    b = pl.program_id(0)
    pr = pl.program_id(1)
    page = v_refs[0].shape[1]
    pg_per_blk = MOBA_BLOCK // page
    lane = lax.broadcasted_iota(jnp.int32, (SUBLANES, LANES), 1)
    pair = jnp.zeros((SUBLANES, LANES), F32)
    for hf in range(2):
        h = 2 * pr + hf
        acc = pself_ref[pl.ds(h, 1), :] * vs_ref[...]
        acc = jnp.broadcast_to(acc, (SUBLANES, LANES))
        for r in range(MOBA_TOPK):
            blk = sel_ref[(b * H_MOBA + h) * MOBA_TOPK + r]
            for pg in range(pg_per_blk):
                pp = jnp.broadcast_to(p_ref[hf, pl.ds(blk * pg_per_blk + pg, 1), :], (SUBLANES, page))
                vt = v_refs[(hf * MOBA_TOPK + r) * pg_per_blk + pg][...]
                acc = acc + _dot_nt(pp.astype(BF16), vt.astype(BF16))
        pair = jnp.where(lane // HALF == hf, acc, pair)
    o_ref[...] = pair[0:1]


def _moba_values(page_table, sel, p, pself, v_self, cache_vt, *, layer_off):
    nb, n_pages = page_table.shape
    page = cache_vt.shape[2]
    pg_per_blk = MOBA_BLOCK // page
    n_pair = H_MOBA // 2
    n_chunk = 2 * MOBA_TOPK * pg_per_blk

    def chunk_spec(hf, r, pg):
        def index_map(b, pr, pt, sl):
            blk = sl[(b * H_MOBA + 2 * pr + hf) * MOBA_TOPK + r]
            return (layer_off + pt[b * n_pages + blk * pg_per_blk + pg], pr, 0)
        return pl.BlockSpec((None, LANES, page), index_map)

    pair_vec = pl.BlockSpec((None, 1, LANES), lambda b, pr, pt, sl: (b, 0, pr))
    grid_spec = pltpu.PrefetchScalarGridSpec(
        num_scalar_prefetch=2,
        grid=(nb, n_pair),
        in_specs=[pl.BlockSpec((None, 2, n_pages, page), lambda b, pr, pt, sl: (b, pr, 0, 0)),
                  pl.BlockSpec((None, H_MOBA, LANES), lambda b, pr, pt, sl: (b, 0, 0)),
                  pair_vec]
                 + [chunk_spec(hf, r, pg) for hf in range(2) for r in range(MOBA_TOPK)
                    for pg in range(pg_per_blk)],
        out_specs=pair_vec,
    )
    out = pl.pallas_call(
        functools.partial(_moba_values_kernel, n_chunk=n_chunk),
        grid_spec=grid_spec,
        out_shape=jax.ShapeDtypeStruct((nb, 1, GROUP), F32),
        compiler_params=_params(("arbitrary", "arbitrary")),
    )(page_table.reshape(-1), sel.reshape(-1), p, pself, v_self.reshape(nb, 1, GROUP),
      *([cache_vt] * n_chunk))
    return out.reshape(nb, GROUP)


PROMPT_TILE = 512
ATTN_TILE = 512
DIFF_DECODE_PAGES = 16
MOBA_DECODE_PAGES = 32


def _feature_major(cache, n_lead):
    nd = cache.ndim
    perm = (0, 1) + tuple(range(3, nd)) + (2,)
    t = jnp.transpose(cache, perm)
    return t.reshape(n_lead, -1, cache.shape[2])


def kernel(x_prompt, x_sample, cache_diff_k, cache_diff_v, cache_moba_k, cache_moba_v, cache_mem_k, cache_mem_v, page_table, mem_prompt, g_pre_mix, w_in, lambda_q1, lambda_k1, lambda_q2, lambda_k2, g_subln, w_out, g_post_mix, g_mem, w_mem_k, w_mem_v, g_pre_x, w_xq, w_xo, g_post_x, g_pre_ff, w_up, w_down, g_post_ff):
    nb_p, seq, d = x_prompt.shape
    nb_s, s_len, _ = x_sample.shape
    assert s_len == 1, "the sample kernels handle one new token per sequence"
    depth = w_in.shape[0]
    n_pool, page = cache_diff_k.shape[1], cache_diff_k.shape[2]
    assert MOBA_BLOCK % page == 0 and (page_table.shape[1] * page) % MOBA_BLOCK == 0
    n_blk_seq = seq // MOBA_BLOCK
    tm = min(PROMPT_TILE, seq)
    tq = min(ATTN_TILE, seq)

    xp = x_prompt.reshape(nb_p * seq, d)
    xs = x_sample.reshape(nb_s, d)
    ckt_d = _feature_major(cache_diff_k, depth * n_pool)
    cv_d = cache_diff_v.reshape(depth * n_pool, page * H_DIFF, DV_DIFF)
    ckt_m = _feature_major(cache_moba_k, depth * n_pool)
    cvt_m = _feature_major(cache_moba_v, depth * n_pool)
    mem_x = mem_prompt.reshape(nb_p * N_MEM, d)
    slopes_d = _alibi_slopes(H_DIFF)
    slopes_m = _alibi_slopes(H_MOBA)
    ktab = _key_aug_table(seq)
    aug_d = jnp.asarray(_query_aug_table(np.stack([slopes_d, slopes_d], axis=1)))
    aug_m = jnp.asarray(_query_aug_table(slopes_m.reshape(-1, 2)).reshape(1, GROUP))

    outs = [[] for _ in range(10)]
    for l in range(depth):
        lam_init = 0.8 - 0.6 * math.exp(-0.3 * l)
        lamp = jnp.stack([lambda_q1[l], lambda_k1[l], lambda_q2[l], lambda_k2[l]]).astype(F32)
        w_in_bf = w_in[l].astype(BF16)
        w_out_bf = w_out[l].astype(BF16)
        wts = {
            'w_out_a': w_out_bf[:GROUP], 'w_out_b': w_out_bf[GROUP:], 'g_post_mix': g_post_mix[l],
            'g_pre_x': g_pre_x[l], 'w_xq': w_xq[l].astype(BF16), 'w_xo': w_xo[l].astype(BF16),
            'g_post_x': g_post_x[l], 'g_pre_ff': g_pre_ff[l], 'w_up': w_up[l].astype(BF16),
            'w_down': w_down[l].astype(BF16), 'g_post_ff': g_post_ff[l],
        }

        (dq, dk_t, dk_bf, dv, dv_bf, mq, mk_t, mk_bf, mv_t, mv_bf, mq_aug) = _project_prompt(
            xp, g_pre_mix[l], w_in_bf, aug_m, tm=tm, n_blk_seq=n_blk_seq)
        o_d = _pair_attention(dq, aug_d, dk_bf, ktab, dv_bf, nb=nb_p, seq=seq, tq=tq, moba=False,
                              lamp=lamp, g_subln=g_subln[l], lam_init=lam_init)
        o_m = _pair_attention(mq, mq_aug, mk_bf, ktab, mv_bf, nb=nb_p, seq=seq, tq=tq, moba=True)
        mem_tm = min(PROMPT_TILE, nb_p * N_MEM)
        memk = _norm_matmul(mem_x, g_mem[l], w_mem_k[l].astype(BF16), tm=mem_tm, out_dtype=F32)
        memv = _norm_matmul(mem_x, g_mem[l], w_mem_v[l].astype(BF16), tm=mem_tm, out_dtype=F32)
        xp = _finish_layer(xp, o_d, o_m, memk.reshape(nb_p, N_MEM, -1), memv.reshape(nb_p, N_MEM, -1),
                           wts, nb=nb_p, rows=seq, tm=tm)
        outs[0].append(jnp.transpose(dk_t.reshape(nb_p, H_DIFF, 2, DH_DIFF, seq), (0, 4, 1, 2, 3)))
        outs[1].append(dv.reshape(nb_p, seq, H_DIFF, DV_DIFF))
        outs[2].append(jnp.transpose(mk_t.reshape(nb_p, H_MOBA, DH_MOBA, seq), (0, 3, 1, 2)))
        outs[3].append(jnp.transpose(mv_t.reshape(nb_p, H_MOBA, DH_MOBA, seq), (0, 3, 1, 2)))
        outs[4].append(memk.reshape(nb_p, N_MEM, H_MEM, DH_MEM))
        outs[5].append(memv.reshape(nb_p, N_MEM, H_MEM, DH_MEM))

        sq, sk, sv, tq_s, tk, tv = _project_sample(xs, g_pre_mix[l], w_in[l])
        o_ds = _diff_decode(page_table, sq, sk, sv, lamp, g_subln[l], ckt_d, cv_d,
                            layer_off=l * n_pool, n_pg=math.gcd(DIFF_DECODE_PAGES, page_table.shape[1]),
                            lam_init=lam_init)
        p_s, sel_s, pself_s = _moba_scores(page_table, tq_s, tk, ckt_m, layer_off=l * n_pool,
                                           n_pg=math.gcd(MOBA_DECODE_PAGES, page_table.shape[1]))
        o_ms = _moba_values(page_table, sel_s[:, :, :MOBA_TOPK], p_s, pself_s, tv, cvt_m,
                            layer_off=l * n_pool)
        xs = _finish_layer(xs, o_ds, o_ms, cache_mem_k[l].reshape(nb_s, N_MEM, -1),
                           cache_mem_v[l].reshape(nb_s, N_MEM, -1), wts, nb=nb_s, rows=1, tm=nb_s,
                           rows_pad=SUBLANES)
        outs[6].append(sk.reshape(nb_s, 1, H_DIFF, 2, DH_DIFF))
        outs[7].append(sv.reshape(nb_s, 1, H_DIFF, DV_DIFF))
        outs[8].append(tk.reshape(nb_s, 1, H_MOBA, DH_MOBA))
        outs[9].append(tv.reshape(nb_s, 1, H_MOBA, DH_MOBA))

    return (xp.reshape(nb_p, seq, d), xs.reshape(nb_s, 1, d), *[jnp.stack(o) for o in outs])
```

```python
import functools
import math

import numpy as np
import jax
import jax.numpy as jnp
from jax import lax
from jax.experimental import pallas as pl
from jax.experimental.pallas import tpu as pltpu

F32 = jnp.float32
BF16 = jnp.bfloat16

EPS = 1e-6
NEG_INF = -1e30
REMOVED = -3e38
LOG2E = 1.4426950408889634

H_DIFF = 4
DH_DIFF = 64
DV_DIFF = 2 * DH_DIFF
H_MOBA = 8
DH_MOBA = 64
MOBA_BLOCK = 256
MOBA_TOPK = 3
H_MEM = 4
DH_MEM = 128
N_MEM = 256
GROUP = 512
LANES = 128
SUBLANES = 8
HALF = LANES // 2
SEL_STRIDE = LANES // H_MOBA

AUG_SEL = 0
AUG_POS_IN = 16
AUG_POS_BLK = 19
N_SPLIT = 3

VMEM_LIMIT_BYTES = 56 * 1024 * 1024


def _alibi_slopes(n):
    return np.power(2.0, -8.0 * np.arange(1, n + 1) / n).astype(np.float32)


def _split_bf16(x):
    parts, rest = [], np.asarray(x, np.float32)
    for _ in range(N_SPLIT):
        p = rest.astype(BF16).astype(np.float32)
        parts.append(p)
        rest = rest - p
    return parts


def _query_aug_table(slopes_per_half):
    slopes_per_half = np.asarray(slopes_per_half, np.float32)
    tab = np.zeros((slopes_per_half.shape[0], 1, LANES), np.float32)
    for hf in range(2):
        for i, part in enumerate(_split_bf16(slopes_per_half[:, hf] * np.float32(LOG2E))):
            tab[:, 0, hf * HALF + AUG_POS_IN + i] = part
            tab[:, 0, hf * HALF + AUG_POS_BLK + i] = part
    return tab


def _key_aug_table(seq):
    t = np.arange(seq)
    tab = np.zeros((seq, LANES), np.float32)
    for hf in range(2):
        tab[t, hf * HALF + AUG_SEL + t // MOBA_BLOCK] = 1.0
        tab[:, hf * HALF + AUG_POS_IN:hf * HALF + AUG_POS_IN + N_SPLIT] = (t % MOBA_BLOCK)[:, None]
        tab[:, hf * HALF + AUG_POS_BLK:hf * HALF + AUG_POS_BLK + N_SPLIT] = (t - t % MOBA_BLOCK)[:, None]
    return jnp.asarray(tab, BF16)


def _params(sem):
    return pltpu.CompilerParams(dimension_semantics=sem, vmem_limit_bytes=VMEM_LIMIT_BYTES)


def _rms(x, g):
    return x * lax.rsqrt(jnp.mean(x * x, axis=-1, keepdims=True) + EPS) * g


def _dot(a, b):
    return jnp.dot(a, b, preferred_element_type=F32)


def _dot_nt(a, b, precision=None):
    return lax.dot_general(a, b, (((1,), (1,)), ((), ())), preferred_element_type=F32,
                           precision=precision)


def _lambda(lp, lam_init):
    return (jnp.exp(jnp.sum(lp[0:1] * lp[1:2], axis=-1, keepdims=True))
            - jnp.exp(jnp.sum(lp[2:3] * lp[3:4], axis=-1, keepdims=True)) + lam_init)


def _proj_prompt_kernel(x_ref, g_ref, w_ref, atab_ref, dq_ref, dkt_ref, dkb_ref, dv_ref, dvb_ref,
                        mq_ref, mkt_ref, mkb_ref, mvt_ref, mvb_ref, aug_ref, kmt_ref, *, tm, n_blk_seq):
    xn = _rms(x_ref[...], g_ref[...]).astype(BF16)

    def col(c):
        return _dot(xn, w_ref[:, c * GROUP:(c + 1) * GROUP])

    dq_ref[...] = (col(0) * (DH_DIFF ** -0.5 * LOG2E)).astype(BF16)
    z = col(1)
    dkt_ref[...] = z.T
    dkb_ref[...] = z.astype(BF16)
    z = col(2)
    dv_ref[...] = z
    dvb_ref[...] = z.astype(BF16)
    mq = col(3)
    mq_ref[...] = (mq * (DH_MOBA ** -0.5 * LOG2E)).astype(BF16)
    mk = col(4)
    mkt_ref[...] = mk.T
    mkb_ref[...] = mk.astype(BF16)
    z = col(5)
    mvt_ref[...] = z.T
    mvb_ref[...] = z.astype(BF16)

    t = pl.program_id(0)
    blk_per_tile = tm // MOBA_BLOCK
    tiles_per_seq = n_blk_seq // blk_per_tile
    blk0 = (t % tiles_per_seq) * blk_per_tile

    @pl.when(t == 0)
    def _():
        kmt_ref[...] = jnp.zeros_like(kmt_ref)

    lane_g = lax.broadcasted_iota(jnp.int32, (1, GROUP), 1) // DH_MOBA
    for r in range(blk_per_tile):
        km = jnp.mean(mk[r * MOBA_BLOCK:(r + 1) * MOBA_BLOCK], axis=0, keepdims=True)
        for h in range(H_MOBA):
            kmt_ref[pl.ds(h * SEL_STRIDE + blk0 + r, 1), :] = jnp.where(lane_g == h, km, 0.0)

    gate = _dot_nt(mq * (DH_MOBA ** -0.5), kmt_ref[...], precision=lax.Precision.HIGHEST)
    lane = lax.broadcasted_iota(jnp.int32, (tm, LANES), 1)
    row = lax.broadcasted_iota(jnp.int32, (tm, LANES), 0)
    n_in = lane % SEL_STRIDE
    own = blk0 + row // MOBA_BLOCK
    past = n_in < own
    g = jnp.where(past, gate, NEG_INF)
    cnt = jnp.zeros((tm, LANES), jnp.int32)
    for s in range(1, n_blk_seq):
        lo = pltpu.roll(g, s, 1)
        cnt += jnp.where((n_in >= s) & (lo >= g), 1, 0)
        hi = pltpu.roll(g, LANES - s, 1)
        cnt += jnp.where((n_in + s < SEL_STRIDE) & (hi > g), 1, 0)
    visible = (past & (cnt < MOBA_TOPK)) | (n_in == own)
    sel = jnp.where(visible, 0.0, NEG_INF).astype(BF16)
    er = lax.broadcasted_iota(jnp.int32, (LANES, GROUP), 0)
    ec = lax.broadcasted_iota(jnp.int32, (LANES, GROUP), 1)
    spread = jnp.where((ec // HALF == er // SEL_STRIDE) & (ec % HALF == AUG_SEL + er % SEL_STRIDE),
                       1.0, 0.0).astype(BF16)
    aug_ref[...] = (_dot(sel, spread) + atab_ref[...]).astype(BF16)


def _project_prompt(x2d, g, w_bf, aug_tab, *, tm, n_blk_seq):
    n, d = x2d.shape
    assert tm % MOBA_BLOCK == 0 and n_blk_seq % (tm // MOBA_BLOCK) == 0 and n_blk_seq <= SEL_STRIDE
    seq = n_blk_seq * MOBA_BLOCK
    tiles_per_seq = seq // tm
    row_spec = pl.BlockSpec((tm, GROUP), lambda t: (t, 0))
    t_spec = pl.BlockSpec((None, GROUP, tm), lambda t: (t // tiles_per_seq, 0, t % tiles_per_seq))
    f32_out = jax.ShapeDtypeStruct((n, GROUP), F32)
    bf_out = jax.ShapeDtypeStruct((n, GROUP), BF16)
    t_out = jax.ShapeDtypeStruct((n // seq, GROUP, seq), F32)
    return pl.pallas_call(
        functools.partial(_proj_prompt_kernel, tm=tm, n_blk_seq=n_blk_seq),
        grid=(n // tm,),
        in_specs=[pl.BlockSpec((tm, d), lambda t: (t, 0)),
                  pl.BlockSpec((1, d), lambda t: (0, 0)),
                  pl.BlockSpec((d, 6 * GROUP), lambda t: (0, 0)),
                  pl.BlockSpec((1, GROUP), lambda t: (0, 0))],
        out_specs=[row_spec, t_spec, row_spec, row_spec, row_spec, row_spec, t_spec, row_spec, t_spec,
                   row_spec, row_spec],
        out_shape=[bf_out, t_out, bf_out, f32_out, bf_out, bf_out, t_out, bf_out, t_out, bf_out, bf_out],
        scratch_shapes=[pltpu.VMEM((LANES, GROUP), F32)],
        compiler_params=_params(("arbitrary",)),
    )(x2d, g.reshape(1, d), w_bf, aug_tab)


def _proj_sample_kernel(x_ref, g_ref, w_ref, dq_ref, dk_ref, dv_ref, mq_ref, mk_ref, mv_ref):
    xn = _rms(x_ref[...], g_ref[...])
    scales = (DH_DIFF ** -0.5, 1.0, 1.0, DH_MOBA ** -0.5, 1.0, 1.0)
    for c, (ref, sc) in enumerate(zip((dq_ref, dk_ref, dv_ref, mq_ref, mk_ref, mv_ref), scales)):
        ref[...] = jnp.dot(xn, w_ref[:, c * GROUP:(c + 1) * GROUP], preferred_element_type=F32,
                           precision=lax.Precision.HIGHEST) * sc


def _project_sample(x2d, g, w_f32):
    n, d = x2d.shape
    out = jax.ShapeDtypeStruct((n, GROUP), F32)
    return pl.pallas_call(
        _proj_sample_kernel,
        grid=(1,),
        in_specs=[pl.BlockSpec((n, d), lambda t: (0, 0)),
                  pl.BlockSpec((1, d), lambda t: (0, 0)),
                  pl.BlockSpec((d, 6 * GROUP), lambda t: (0, 0))],
        out_specs=[pl.BlockSpec((n, GROUP), lambda t: (0, 0))] * 6,
        out_shape=[out] * 6,
        compiler_params=_params(("arbitrary",)),
    )(x2d, g.reshape(1, d), w_f32)


def _pair_attn_kernel(q_ref, aug_ref, k_ref, ktab_ref, v_ref, *rest, tq, moba, lam_init):
    if moba:
        o_ref, m_scr, acc_scr, s0_scr, s1_scr = rest
    else:
        lamp_ref, g_ref, o_ref, m_scr, acc_scr, s0_scr, s1_scr = rest
    i = pl.program_id(2)

    q = q_ref[...]
    if moba:
        aug = aug_ref[...]
    else:
        aug = jnp.broadcast_to(aug_ref[...], (tq, LANES)).astype(BF16)
    lane = lax.broadcasted_iota(jnp.int32, (tq, LANES), 1)
    lo = lane < HALF
    zero = jnp.zeros_like(q)
    q2 = jnp.concatenate(
        [jnp.concatenate([jnp.where(lo, q, zero), jnp.where(lo, aug, zero)], axis=1),
         jnp.concatenate([jnp.where(lo, zero, q), jnp.where(lo, zero, aug)], axis=1)], axis=0)

    m_scr[...] = jnp.full_like(m_scr, NEG_INF)
    acc_scr[...] = jnp.zeros_like(acc_scr)
    ones = jnp.ones((tq, LANES), BF16)
    n_rep = tq // LANES

    def scores(jb):
        start = pl.multiple_of(jb * tq, tq)
        kj = jnp.concatenate([k_ref[pl.ds(start, tq), :], ktab_ref[pl.ds(start, tq), :]], axis=1)
        return _dot_nt(q2, kj)

    def absorb(s, jb):
        start = pl.multiple_of(jb * tq, tq)
        vj = jnp.concatenate([v_ref[pl.ds(start, tq), :], ones], axis=1)
        m_old = m_scr[...]
        m_new = jnp.maximum(m_old, jnp.max(s, axis=-1, keepdims=True))
        alpha = jnp.exp2(m_old - m_new)
        pe = jnp.exp2(s - jnp.concatenate([m_new] * n_rep, axis=1))
        acc_scr[...] = jnp.concatenate([alpha, alpha], axis=1) * acc_scr[...] + _dot(pe.astype(BF16), vj)
        m_scr[...] = m_new

    r = lax.broadcasted_iota(jnp.int32, (2 * tq, tq), 0)
    c = lax.broadcasted_iota(jnp.int32, (2 * tq, tq), 1)
    s0_scr[...] = scores(0)
    absorb(jnp.where(r % tq >= c, scores(i), NEG_INF), i)
    last = jnp.maximum(i - 1, 0)

    def body(u, carry):
        s1_scr[...] = scores(jnp.minimum(2 * u + 1, last))
        absorb(s0_scr[...], 2 * u)

        @pl.when(2 * u + 1 < i)
        def _():
            s0_scr[...] = scores(jnp.minimum(2 * u + 2, last))
            absorb(s1_scr[...], 2 * u + 1)

        return carry

    lax.fori_loop(0, (i + 1) // 2, body, 0)

    acc = acc_scr[...]
    o2 = acc[:, :LANES] / acc[:, LANES:]
    o0, o1 = o2[:tq], o2[tq:]
    if moba:
        o_ref[...] = jnp.where(lo, o0, o1).astype(o_ref.dtype)
    else:
        o = o0 - _lambda(lamp_ref[...], lam_init) * o1
        o_ref[...] = (_rms(o, g_ref[...]) * (1.0 - lam_init)).astype(o_ref.dtype)


def _pair_attention(q_bf, aug, k_bf, ktab, v_bf, *, nb, seq, tq, moba, lamp=None, g_subln=None,
                    lam_init=0.0):
    n_pair = GROUP // LANES
    nq = seq // tq
    assert tq % MOBA_BLOCK == 0 and seq % tq == 0
    q_spec = pl.BlockSpec((tq, LANES), lambda b, p, i: (b * nq + i, p))
    kv_spec = pl.BlockSpec((seq, LANES), lambda b, p, i: (b, p))
    if moba:
        aug_spec = q_spec
    else:
        aug_spec = pl.BlockSpec((None, 1, LANES), lambda b, p, i: (p, 0, 0))
    in_specs = [q_spec, aug_spec, kv_spec, pl.BlockSpec((seq, LANES), lambda b, p, i: (0, 0)), kv_spec]
    args = [q_bf, aug, k_bf, ktab, v_bf]
    if not moba:
        in_specs += [pl.BlockSpec((4, DH_DIFF), lambda b, p, i: (0, 0)),
                     pl.BlockSpec((1, DV_DIFF), lambda b, p, i: (0, 0))]
        args += [lamp, g_subln.reshape(1, DV_DIFF)]
    return pl.pallas_call(
        functools.partial(_pair_attn_kernel, tq=tq, moba=moba, lam_init=lam_init),
        grid=(nb, n_pair, nq),
        in_specs=in_specs,
        out_specs=q_spec,
        out_shape=jax.ShapeDtypeStruct((nb * seq, GROUP), BF16),
        scratch_shapes=[pltpu.VMEM((2 * tq, LANES), F32), pltpu.VMEM((2 * tq, 2 * LANES), F32),
                        pltpu.VMEM((2 * tq, tq), F32), pltpu.VMEM((2 * tq, tq), F32)],
        compiler_params=_params(("arbitrary", "arbitrary", "arbitrary")),
    )(*args)


def _norm_mm_kernel(x_ref, g_ref, w_ref, o_ref, *, scale, relu2):
    y = _dot(_rms(x_ref[...], g_ref[...]).astype(BF16), w_ref[...])
    if relu2:
        y = jnp.square(jnp.maximum(y, 0.0))
    if scale != 1.0:
        y = y * scale
    o_ref[...] = y.astype(o_ref.dtype)


def _norm_matmul(x2d, g, w_bf, *, tm, out_dtype, scale=1.0, relu2=False):
    n, d = x2d.shape
    e = w_bf.shape[1]
    return pl.pallas_call(
        functools.partial(_norm_mm_kernel, scale=scale, relu2=relu2),
        grid=(n // tm,),
        in_specs=[pl.BlockSpec((tm, d), lambda t: (t, 0)),
                  pl.BlockSpec((1, d), lambda t: (0, 0)),
                  pl.BlockSpec((d, e), lambda t: (0, 0))],
        out_specs=pl.BlockSpec((tm, e), lambda t: (t, 0)),
        out_shape=jax.ShapeDtypeStruct((n, e), out_dtype),
        compiler_params=_params(("arbitrary",)),
    )(x2d, g.reshape(1, d), w_bf)


def _mm_norm_res_kernel(*refs, n_in):
    a_refs = refs[:n_in]
    w_refs = refs[n_in:2 * n_in]
    x_ref, g_ref, o_ref = refs[2 * n_in:]
    y = _dot(a_refs[0][...].astype(BF16), w_refs[0][...])
    for a_ref, w_ref in zip(a_refs[1:], w_refs[1:]):
        y = y + _dot(a_ref[...].astype(BF16), w_ref[...])
    o_ref[...] = x_ref[...] + _rms(y, g_ref[...])


def _matmul_norm_residual(a_list, w_list, x2d, g, *, tm):
    n, d = x2d.shape
    n_in = len(a_list)
    in_specs = [pl.BlockSpec((tm, a.shape[1]), lambda t: (t, 0)) for a in a_list]
    in_specs += [pl.BlockSpec(w.shape, lambda t: (0, 0)) for w in w_list]
    in_specs += [pl.BlockSpec((tm, d), lambda t: (t, 0)), pl.BlockSpec((1, d), lambda t: (0, 0))]
    return pl.pallas_call(
        functools.partial(_mm_norm_res_kernel, n_in=n_in),
        grid=(n // tm,),
        in_specs=in_specs,
        out_specs=pl.BlockSpec((tm, d), lambda t: (t, 0)),
        out_shape=jax.ShapeDtypeStruct((n, d), F32),
        compiler_params=_params(("arbitrary",)),
    )(*a_list, *w_list, x2d, g.reshape(1, d))


def _mem_attn_kernel(q_ref, k_ref, v_ref, o_ref, *, interleaved):
    q = q_ref[...]
    outs = []
    for h in range(H_MEM):
        cols = slice(h * DH_MEM, (h + 1) * DH_MEM)
        if interleaved:
            kh = k_ref[pl.ds(h, N_MEM, stride=H_MEM), :]
            vh = v_ref[pl.ds(h, N_MEM, stride=H_MEM), :]
        else:
            kh, vh = k_ref[:, cols], v_ref[:, cols]
        s = _dot_nt(q[:, cols], kh.astype(BF16))
        pe = jnp.exp(s - jnp.max(s, axis=-1, keepdims=True))
        o = _dot(pe.astype(BF16), vh.astype(BF16))
        outs.append(o / jnp.sum(pe, axis=-1, keepdims=True))
    o_ref[...] = jnp.concatenate(outs, axis=1).astype(o_ref.dtype)


def _mem_attention(q_bf, mem_k, mem_v, *, nb, rows, tm):
    d = H_MEM * DH_MEM
    nt = rows // tm
    interleaved = mem_k.shape[1] == N_MEM * H_MEM
    mem_block = (None,) + mem_k.shape[1:]
    return pl.pallas_call(
        functools.partial(_mem_attn_kernel, interleaved=interleaved),
        grid=(nb, nt),
        in_specs=[pl.BlockSpec((tm, d), lambda b, t: (b * nt + t, 0)),
                  pl.BlockSpec(mem_block, lambda b, t: (b, 0, 0)),
                  pl.BlockSpec(mem_block, lambda b, t: (b, 0, 0))],
        out_specs=pl.BlockSpec((tm, d), lambda b, t: (b * nt + t, 0)),
        out_shape=jax.ShapeDtypeStruct((nb * rows, d), BF16),
        compiler_params=_params(("arbitrary", "arbitrary")),
    )(q_bf, mem_k, mem_v)


def _mlp_kernel(x_ref, g_pre_ref, w_up_ref, w_down_ref, g_post_ref, o_ref):
    x = x_ref[...]
    h = _dot(_rms(x, g_pre_ref[...]).astype(BF16), w_up_ref[...])
    h = jnp.square(jnp.maximum(h, 0.0)).astype(BF16)
    o_ref[...] = x + _rms(_dot(h, w_down_ref[...]), g_post_ref[...])


def _mlp(x2d, g_pre, w_up, w_down, g_post, *, tm):
    n, d = x2d.shape
    f = w_up.shape[1]
    const = lambda t: (0, 0)
    resident = pl.Buffered(1)
    return pl.pallas_call(
        _mlp_kernel,
        grid=(n // tm,),
        in_specs=[pl.BlockSpec((tm, d), lambda t: (t, 0)),
                  pl.BlockSpec((1, d), const),
                  pl.BlockSpec((d, f), const, pipeline_mode=resident),
                  pl.BlockSpec((f, d), const, pipeline_mode=resident),
                  pl.BlockSpec((1, d), const)],
        out_specs=pl.BlockSpec((tm, d), lambda t: (t, 0)),
        out_shape=jax.ShapeDtypeStruct((n, d), F32),
        compiler_params=_params(("arbitrary",)),
    )(x2d, g_pre.reshape(1, d), w_up, w_down, g_post.reshape(1, d))


def _finish_layer(x2d, o_diff, o_moba, mem_k, mem_v, wts, *, nb, rows, tm, rows_pad=1):
    x1 = _matmul_norm_residual([o_diff, o_moba], [wts['w_out_a'], wts['w_out_b']], x2d,
                               wts['g_post_mix'], tm=tm)
    q = _norm_matmul(x1, wts['g_pre_x'], wts['w_xq'], tm=tm, out_dtype=BF16, scale=DH_MEM ** -0.5)
    if rows_pad > 1:
        qp = jnp.broadcast_to(q[:, None, :], (nb * rows, rows_pad, q.shape[1])).reshape(-1, q.shape[1])
        c = _mem_attention(qp, mem_k, mem_v, nb=nb, rows=rows * rows_pad, tm=rows * rows_pad)
        c = c.reshape(nb * rows, rows_pad, -1)[:, 0, :]
    else:
        c = _mem_attention(q, mem_k, mem_v, nb=nb, rows=rows, tm=min(tm, rows))
    x2 = _matmul_norm_residual([c], [wts['w_xo']], x1, wts['g_post_x'], tm=tm)
    return _mlp(x2, wts['g_pre_ff'], wts['w_up'], wts['w_down'], wts['g_post_ff'], tm=tm)


def _query_columns(q_row):
    r = lax.broadcasted_iota(jnp.int32, (GROUP, GROUP), 0)
    c = lax.broadcasted_iota(jnp.int32, (GROUP, GROUP), 1)
    col = jnp.sum(jnp.where(r == c, jnp.broadcast_to(q_row, (GROUP, GROUP)), 0.0), axis=-1, keepdims=True)
    return jnp.broadcast_to(col, (GROUP, LANES))


def _page_scores(kt, qb, n_grp):
    prod = kt * qb
    w = GROUP // n_grp
    return jnp.concatenate([jnp.sum(prod[g * w:(g + 1) * w], axis=0, keepdims=True) for g in range(n_grp)],
                           axis=0)


def _self_scores(q_row, k_row, n_grp):
    row = lax.broadcasted_iota(jnp.int32, (n_grp, GROUP), 0)
    lane = lax.broadcasted_iota(jnp.int32, (n_grp, GROUP), 1)
    prod = jnp.broadcast_to(q_row * k_row, (n_grp, GROUP))
    return jnp.sum(jnp.where(lane // (GROUP // n_grp) == row, prod, 0.0), axis=-1, keepdims=True)


def _diff_decode_kernel(pt_ref, q_ref, ks_ref, vs_ref, lamp_ref, g_ref, *rest, n_pg, past, lam_init):
    k_refs = rest[:n_pg]
    v_refs = rest[n_pg:2 * n_pg]
    o_ref, qb_scr, m_scr, l_scr, acc_scr = rest[2 * n_pg:]
    step = pl.program_id(1)
    n_rows = 2 * H_DIFF
    page = k_refs[0].shape[1]
    slopes = _alibi_slopes(H_DIFF)
    row1 = lax.broadcasted_iota(jnp.int32, (n_rows, 1), 0)
    slope_col = jnp.zeros((n_rows, 1), F32)
    for h in range(H_DIFF):
        slope_col = jnp.where(row1 // 2 == h, float(slopes[h]), slope_col)

    @pl.when(step == 0)
    def _():
        qb_scr[...] = _query_columns(q_ref[...])
        m_scr[...] = jnp.broadcast_to(_self_scores(q_ref[...], ks_ref[...], n_rows), (n_rows, LANES))
        l_scr[...] = jnp.ones_like(l_scr)
        for h in range(H_DIFF):
            acc_scr[h] = jnp.broadcast_to(vs_ref[:, h * DV_DIFF:(h + 1) * DV_DIFF], (n_rows, DV_DIFF))

    qb = qb_scr[...]
    s = jnp.concatenate([_page_scores(k_refs[pg][...], qb, n_rows) for pg in range(n_pg)], axis=1)
    kpos = step * (n_pg * page) + lax.broadcasted_iota(jnp.int32, (1, n_pg * page), 1)
    s = s - slope_col * (past - kpos).astype(F32)
    m_old = m_scr[...]
    m_new = jnp.maximum(m_old, jnp.max(s, axis=-1, keepdims=True))
    alpha = jnp.exp(m_old - m_new)
    pe = jnp.exp(s - jnp.concatenate([m_new] * n_pg, axis=1))
    l_scr[...] = alpha * l_scr[...] + jnp.sum(pe, axis=-1, keepdims=True)
    pe = pe.astype(BF16)
    for h in range(H_DIFF):
        pv = None
        for pg in range(n_pg):
            vh = v_refs[pg][pl.ds(h, page, stride=H_DIFF), :].astype(BF16)
            d = _dot(pe[:, pg * page:(pg + 1) * page], vh)
            pv = d if pv is None else pv + d
        acc_scr[h] = alpha * acc_scr[h] + pv
    m_scr[...] = m_new

    @pl.when(step == pl.num_programs(1) - 1)
    def _():
        lam = _lambda(lamp_ref[...], lam_init)
        row = lax.broadcasted_iota(jnp.int32, (n_rows, DV_DIFF), 0)
        inv_l = 1.0 / l_scr[...]
        parts = []
        for h in range(H_DIFF):
            coef = jnp.where(row == 2 * h, 1.0, jnp.where(row == 2 * h + 1, -lam, 0.0))
            o = jnp.sum(acc_scr[h] * inv_l * coef, axis=0, keepdims=True)
            parts.append(_rms(o, g_ref[...]))
        o_ref[...] = jnp.concatenate(parts, axis=1) * (1.0 - lam_init)


def _diff_decode(page_table, q, k_self, v_self, lamp, g_subln, cache_kt, cache_v, *,
                 layer_off, n_pg, lam_init):
    nb, n_pages = page_table.shape
    page = cache_kt.shape[2]
    past = n_pages * page
    assert n_pages % n_pg == 0 and page == LANES

    def page_spec(pg, rows, cols):
        return pl.BlockSpec((None, rows, cols),
                            lambda b, s, pt: (layer_off + pt[b * n_pages + s * n_pg + pg], 0, 0))

    vec_spec = pl.BlockSpec((None, 1, GROUP), lambda b, s, pt: (b, 0, 0))
    n_rows = 2 * H_DIFF
    grid_spec = pltpu.PrefetchScalarGridSpec(
        num_scalar_prefetch=1,
        grid=(nb, n_pages // n_pg),
        in_specs=[vec_spec, vec_spec, vec_spec,
                  pl.BlockSpec((4, DH_DIFF), lambda b, s, pt: (0, 0)),
                  pl.BlockSpec((1, DV_DIFF), lambda b, s, pt: (0, 0))]
                 + [page_spec(pg, GROUP, page) for pg in range(n_pg)]
                 + [page_spec(pg, page * H_DIFF, DV_DIFF) for pg in range(n_pg)],
        out_specs=vec_spec,
        scratch_shapes=[pltpu.VMEM((GROUP, LANES), F32), pltpu.VMEM((n_rows, LANES), F32),
                        pltpu.VMEM((n_rows, LANES), F32), pltpu.VMEM((H_DIFF, n_rows, DV_DIFF), F32)],
    )
    out = pl.pallas_call(
        functools.partial(_diff_decode_kernel, n_pg=n_pg, past=past, lam_init=lam_init),
        grid_spec=grid_spec,
        out_shape=jax.ShapeDtypeStruct((nb, 1, GROUP), F32),
        compiler_params=_params(("arbitrary", "arbitrary")),
    )(page_table.reshape(-1), q.reshape(nb, 1, GROUP), k_self.reshape(nb, 1, GROUP),
      v_self.reshape(nb, 1, GROUP), lamp, g_subln.reshape(1, DV_DIFF),
      *([cache_kt] * n_pg), *([cache_v] * n_pg))
    return out.reshape(nb, GROUP)


def _moba_scores_kernel(pt_ref, q_ref, ks_ref, *rest, n_pg, n_pages, past):
    k_refs = rest[:n_pg]
    p_ref, sel_ref, pself_ref, qb_scr, s_scr = rest[n_pg:]
    step = pl.program_id(1)
    page = k_refs[0].shape[1]

    @pl.when(step == 0)
    def _():
        qb_scr[...] = _query_columns(q_ref[...])

    qb = qb_scr[...]
    for pg in range(n_pg):
        s_scr[step * n_pg + pg] = _page_scores(k_refs[pg][...], qb, H_MOBA)

    @pl.when(step == pl.num_programs(1) - 1)
    def _():
        slopes = _alibi_slopes(H_MOBA)
        s_self = _self_scores(q_ref[...], ks_ref[...], H_MOBA)
        pg_i = lax.broadcasted_iota(jnp.int32, (n_pages, page), 0)
        pg_f = pg_i.astype(F32)
        blk_f = jnp.floor(pg_f * 0.5)
        dist = (past - (pg_i * page + lax.broadcasted_iota(jnp.int32, (n_pages, page), 1))).astype(F32)
        row = lax.broadcasted_iota(jnp.int32, (H_MOBA, LANES), 0)
        lane = lax.broadcasted_iota(jnp.int32, (H_MOBA, LANES), 1)
        sel_out = jnp.zeros((H_MOBA, LANES), F32)
        pself_out = jnp.zeros((H_MOBA, LANES), F32)
        for h in range(H_MOBA):
            sh = s_scr[:, h, :]
            rs = jnp.broadcast_to(jnp.sum(sh, axis=-1, keepdims=True), (n_pages, page))
            other = jnp.where(pg_i % 2 == 0, pltpu.roll(rs, n_pages - 1, 0), pltpu.roll(rs, 1, 0))
            g = (rs + other) * (1.0 / MOBA_BLOCK)
            chosen = jnp.zeros((n_pages, page), jnp.bool_)
            for r in range(MOBA_TOPK):
                mx = jnp.max(g, axis=0, keepdims=True)
                first = jnp.min(jnp.where(g == mx, pg_f, float(n_pages)), axis=0, keepdims=True)
                idx = jnp.floor(first * 0.5)
                pick = blk_f == idx
                chosen = chosen | pick
                g = jnp.where(pick, REMOVED, g)
                sel_out = jnp.where((row == h) & (lane == r), idx, sel_out)
            logit = jnp.where(chosen, sh - float(slopes[h]) * dist, NEG_INF)
            ss = s_self[h:h + 1]
            m = jnp.maximum(jnp.max(jnp.max(logit, axis=-1, keepdims=True), axis=0, keepdims=True), ss)
            pe = jnp.exp(logit - m)
            ps = jnp.exp(ss - m)
            inv = 1.0 / (jnp.sum(jnp.sum(pe, axis=-1, keepdims=True), axis=0, keepdims=True) + ps)
            p_ref[h] = pe * inv
            pself_out = jnp.where(row == h, ps * inv, pself_out)
        sel_ref[...] = sel_out.astype(jnp.int32)
        pself_ref[...] = pself_out


def _moba_scores(page_table, q, k_self, cache_kt, *, layer_off, n_pg):
    nb, n_pages = page_table.shape
    page = cache_kt.shape[2]
    past = n_pages * page
    n_blk = past // MOBA_BLOCK
    pg_per_blk = MOBA_BLOCK // page
    assert n_blk >= MOBA_TOPK and n_pages % n_pg == 0 and pg_per_blk == 2 and page == LANES

    def page_spec(pg):
        return pl.BlockSpec((None, GROUP, page),
                            lambda b, s, pt: (layer_off + pt[b * n_pages + s * n_pg + pg], 0, 0))

    vec_spec = pl.BlockSpec((None, 1, GROUP), lambda b, s, pt: (b, 0, 0))
    small_spec = pl.BlockSpec((None, H_MOBA, LANES), lambda b, s, pt: (b, 0, 0))
    grid_spec = pltpu.PrefetchScalarGridSpec(
        num_scalar_prefetch=1,
        grid=(nb, n_pages // n_pg),
        in_specs=[vec_spec, vec_spec] + [page_spec(pg) for pg in range(n_pg)],
        out_specs=[pl.BlockSpec((None, H_MOBA, n_pages, page), lambda b, s, pt: (b, 0, 0, 0)),
                   small_spec, small_spec],
        scratch_shapes=[pltpu.VMEM((GROUP, LANES), F32), pltpu.VMEM((n_pages, H_MOBA, page), F32)],
    )
    return pl.pallas_call(
        functools.partial(_moba_scores_kernel, n_pg=n_pg, n_pages=n_pages, past=past),
        grid_spec=grid_spec,
        out_shape=[jax.ShapeDtypeStruct((nb, H_MOBA, n_pages, page), F32),
                   jax.ShapeDtypeStruct((nb, H_MOBA, LANES), jnp.int32),
                   jax.ShapeDtypeStruct((nb, H_MOBA, LANES), F32)],
        compiler_params=_params(("arbitrary", "arbitrary")),
    )(page_table.reshape(-1), q.reshape(nb, 1, GROUP), k_self.reshape(nb, 1, GROUP),
      *([cache_kt] * n_pg))


def _moba_values_kernel(pt_ref, sel_ref, p_ref, pself_ref, vs_ref, *rest, n_chunk):
    v_refs = rest[:n_chunk]
    o_ref = rest[n_chunk]
    b = pl.program_id(0)
    h0 = pl.program_id(1) * (2 * VALUE_PAIRS)
    page = v_refs[0].shape[1]
    pg_per_blk = MOBA_BLOCK // page
    lane = lax.broadcasted_iota(jnp.int32, (SUBLANES, LANES), 1)
    outs = []
    for pp_i in range(VALUE_PAIRS):
        pair = jnp.zeros((SUBLANES, LANES), F32)
        for hf in range(2):
            hl = 2 * pp_i + hf
            h = h0 + hl
            acc = pself_ref[pl.ds(h, 1), :] * vs_ref[:, pp_i * LANES:(pp_i + 1) * LANES]
            acc = jnp.broadcast_to(acc, (SUBLANES, LANES))
            for r in range(MOBA_TOPK):
                blk = sel_ref[(b * H_MOBA + h) * MOBA_TOPK + r]
                for pg in range(pg_per_blk):
                    pp = jnp.broadcast_to(p_ref[hl, pl.ds(blk * pg_per_blk + pg, 1), :], (SUBLANES, page))
                    vt = v_refs[(hl * MOBA_TOPK + r) * pg_per_blk + pg][...]
                    acc = acc + _dot_nt(pp.astype(BF16), vt.astype(BF16))
            pair = jnp.where(lane // HALF == hf, acc, pair)
        outs.append(pair[0:1])
    o_ref[...] = jnp.concatenate(outs, axis=1)


def _moba_values(page_table, sel, p, pself, v_self, cache_vt, *, layer_off):
    nb, n_pages = page_table.shape
    page = cache_vt.shape[2]
    pg_per_blk = MOBA_BLOCK // page
    heads_per_step = 2 * VALUE_PAIRS
    n_step = H_MOBA // heads_per_step
    n_chunk = heads_per_step * MOBA_TOPK * pg_per_blk

    def chunk_spec(hl, r, pg):
        def index_map(b, st, pt, sl):
            blk = sl[(b * H_MOBA + st * heads_per_step + hl) * MOBA_TOPK + r]
            return (layer_off + pt[b * n_pages + blk * pg_per_blk + pg], st * VALUE_PAIRS + hl // 2, 0)
        return pl.BlockSpec((None, LANES, page), index_map)

    pair_vec = pl.BlockSpec((None, 1, VALUE_PAIRS * LANES), lambda b, st, pt, sl: (b, 0, st))
    grid_spec = pltpu.PrefetchScalarGridSpec(
        num_scalar_prefetch=2,
        grid=(nb, n_step),
        in_specs=[pl.BlockSpec((None, heads_per_step, n_pages, page), lambda b, st, pt, sl: (b, st, 0, 0)),
                  pl.BlockSpec((None, H_MOBA, LANES), lambda b, st, pt, sl: (b, 0, 0)),
                  pair_vec]
                 + [chunk_spec(hl, r, pg) for hl in range(heads_per_step) for r in range(MOBA_TOPK)
                    for pg in range(pg_per_blk)],
        out_specs=pair_vec,
    )
    out = pl.pallas_call(
        functools.partial(_moba_values_kernel, n_chunk=n_chunk),
        grid_spec=grid_spec,
        out_shape=jax.ShapeDtypeStruct((nb, 1, GROUP), F32),
        compiler_params=_params(("arbitrary", "arbitrary")),
    )(page_table.reshape(-1), sel.reshape(-1), p, pself, v_self.reshape(nb, 1, GROUP),
      *([cache_vt] * n_chunk))
    return out.reshape(nb, GROUP)


PROMPT_TILE = 512
ATTN_TILE = 512
DIFF_DECODE_PAGES = 32
MOBA_DECODE_PAGES = 64
VALUE_PAIRS = 2


def _feature_major(cache, n_lead):
    nd = cache.ndim
    perm = (0, 1) + tuple(range(3, nd)) + (2,)
    t = jnp.transpose(cache, perm)
    return t.reshape(n_lead, -1, cache.shape[2])


def kernel(x_prompt, x_sample, cache_diff_k, cache_diff_v, cache_moba_k, cache_moba_v, cache_mem_k, cache_mem_v, page_table, mem_prompt, g_pre_mix, w_in, lambda_q1, lambda_k1, lambda_q2, lambda_k2, g_subln, w_out, g_post_mix, g_mem, w_mem_k, w_mem_v, g_pre_x, w_xq, w_xo, g_post_x, g_pre_ff, w_up, w_down, g_post_ff):
    nb_p, seq, d = x_prompt.shape
    nb_s, s_len, _ = x_sample.shape
    assert s_len == 1, "the sample kernels handle one new token per sequence"
    depth = w_in.shape[0]
    n_pool, page = cache_diff_k.shape[1], cache_diff_k.shape[2]
    assert MOBA_BLOCK % page == 0 and (page_table.shape[1] * page) % MOBA_BLOCK == 0
    n_blk_seq = seq // MOBA_BLOCK
    tm = min(PROMPT_TILE, seq)
    tq = min(ATTN_TILE, seq)

    xp = x_prompt.reshape(nb_p * seq, d)
    xs = x_sample.reshape(nb_s, d)
    ckt_d = _feature_major(cache_diff_k, depth * n_pool)
    cv_d = cache_diff_v.reshape(depth * n_pool, page * H_DIFF, DV_DIFF)
    ckt_m = _feature_major(cache_moba_k, depth * n_pool)
    cvt_m = _feature_major(cache_moba_v, depth * n_pool)
    mem_x = mem_prompt.reshape(nb_p * N_MEM, d)
    slopes_d = _alibi_slopes(H_DIFF)
    slopes_m = _alibi_slopes(H_MOBA)
    ktab = _key_aug_table(seq)
    aug_d = jnp.asarray(_query_aug_table(np.stack([slopes_d, slopes_d], axis=1)))
    aug_m = jnp.asarray(_query_aug_table(slopes_m.reshape(-1, 2)).reshape(1, GROUP))

    outs = [[] for _ in range(10)]
    for l in range(depth):
        lam_init = 0.8 - 0.6 * math.exp(-0.3 * l)
        lamp = jnp.stack([lambda_q1[l], lambda_k1[l], lambda_q2[l], lambda_k2[l]]).astype(F32)
        w_in_bf = w_in[l].astype(BF16)
        w_out_bf = w_out[l].astype(BF16)
        wts = {
            'w_out_a': w_out_bf[:GROUP], 'w_out_b': w_out_bf[GROUP:], 'g_post_mix': g_post_mix[l],
            'g_pre_x': g_pre_x[l], 'w_xq': w_xq[l].astype(BF16), 'w_xo': w_xo[l].astype(BF16),
            'g_post_x': g_post_x[l], 'g_pre_ff': g_pre_ff[l], 'w_up': w_up[l].astype(BF16),
            'w_down': w_down[l].astype(BF16), 'g_post_ff': g_post_ff[l],
        }

        (dq, dk_t, dk_bf, dv, dv_bf, mq, mk_t, mk_bf, mv_t, mv_bf, mq_aug) = _project_prompt(
            xp, g_pre_mix[l], w_in_bf, aug_m, tm=tm, n_blk_seq=n_blk_seq)
        o_d = _pair_attention(dq, aug_d, dk_bf, ktab, dv_bf, nb=nb_p, seq=seq, tq=tq, moba=False,
                              lamp=lamp, g_subln=g_subln[l], lam_init=lam_init)
        o_m = _pair_attention(mq, mq_aug, mk_bf, ktab, mv_bf, nb=nb_p, seq=seq, tq=tq, moba=True)
        mem_tm = min(PROMPT_TILE, nb_p * N_MEM)
        memk = _norm_matmul(mem_x, g_mem[l], w_mem_k[l].astype(BF16), tm=mem_tm, out_dtype=F32)
        memv = _norm_matmul(mem_x, g_mem[l], w_mem_v[l].astype(BF16), tm=mem_tm, out_dtype=F32)
        xp = _finish_layer(xp, o_d, o_m, memk.reshape(nb_p, N_MEM, -1), memv.reshape(nb_p, N_MEM, -1),
                           wts, nb=nb_p, rows=seq, tm=tm)
        outs[0].append(jnp.transpose(dk_t.reshape(nb_p, H_DIFF, 2, DH_DIFF, seq), (0, 4, 1, 2, 3)))
        outs[1].append(dv.reshape(nb_p, seq, H_DIFF, DV_DIFF))
        outs[2].append(jnp.transpose(mk_t.reshape(nb_p, H_MOBA, DH_MOBA, seq), (0, 3, 1, 2)))
        outs[3].append(jnp.transpose(mv_t.reshape(nb_p, H_MOBA, DH_MOBA, seq), (0, 3, 1, 2)))
        outs[4].append(memk.reshape(nb_p, N_MEM, H_MEM, DH_MEM))
        outs[5].append(memv.reshape(nb_p, N_MEM, H_MEM, DH_MEM))

        sq, sk, sv, tq_s, tk, tv = _project_sample(xs, g_pre_mix[l], w_in[l])
        o_ds = _diff_decode(page_table, sq, sk, sv, lamp, g_subln[l], ckt_d, cv_d,
                            layer_off=l * n_pool, n_pg=math.gcd(DIFF_DECODE_PAGES, page_table.shape[1]),
                            lam_init=lam_init)
        p_s, sel_s, pself_s = _moba_scores(page_table, tq_s, tk, ckt_m, layer_off=l * n_pool,
                                           n_pg=math.gcd(MOBA_DECODE_PAGES, page_table.shape[1]))
        o_ms = _moba_values(page_table, sel_s[:, :, :MOBA_TOPK], p_s, pself_s, tv, cvt_m,
                            layer_off=l * n_pool)
        xs = _finish_layer(xs, o_ds, o_ms, cache_mem_k[l].reshape(nb_s, N_MEM * H_MEM, DH_MEM),
                           cache_mem_v[l].reshape(nb_s, N_MEM * H_MEM, DH_MEM), wts, nb=nb_s, rows=1, tm=nb_s,
                           rows_pad=SUBLANES)
        outs[6].append(sk.reshape(nb_s, 1, H_DIFF, 2, DH_DIFF))
        outs[7].append(sv.reshape(nb_s, 1, H_DIFF, DV_DIFF))
        outs[8].append(tk.reshape(nb_s, 1, H_MOBA, DH_MOBA))
        outs[9].append(tv.reshape(nb_s, 1, H_MOBA, DH_MOBA))

    return (xp.reshape(nb_p, seq, d), xs.reshape(nb_s, 1, d), *[jnp.stack(o) for o in outs])
```

```python
import functools
import math

import numpy as np
import jax
import jax.numpy as jnp
from jax import lax
from jax.experimental import pallas as pl
from jax.experimental.pallas import tpu as pltpu

F32 = jnp.float32
BF16 = jnp.bfloat16

EPS = 1e-6
NEG_INF = -1e30
REMOVED = -3e38
LOG2E = 1.4426950408889634

H_DIFF = 4
DH_DIFF = 64
DV_DIFF = 2 * DH_DIFF
H_MOBA = 8
DH_MOBA = 64
MOBA_BLOCK = 256
MOBA_TOPK = 3
H_MEM = 4
DH_MEM = 128
N_MEM = 256
GROUP = 512
LANES = 128
SUBLANES = 8
HALF = LANES // 2
SEL_STRIDE = LANES // H_MOBA

AUG_SEL = 0
AUG_POS_IN = 16
AUG_POS_BLK = 19
N_SPLIT = 3

VMEM_LIMIT_BYTES = 56 * 1024 * 1024


def _alibi_slopes(n):
    return np.power(2.0, -8.0 * np.arange(1, n + 1) / n).astype(np.float32)


def _split_bf16(x):
    parts, rest = [], np.asarray(x, np.float32)
    for _ in range(N_SPLIT):
        p = rest.astype(BF16).astype(np.float32)
        parts.append(p)
        rest = rest - p
    return parts


def _query_aug_table(slopes_per_half):
    slopes_per_half = np.asarray(slopes_per_half, np.float32)
    tab = np.zeros((slopes_per_half.shape[0], 1, LANES), np.float32)
    for hf in range(2):
        for i, part in enumerate(_split_bf16(slopes_per_half[:, hf] * np.float32(LOG2E))):
            tab[:, 0, hf * HALF + AUG_POS_IN + i] = part
            tab[:, 0, hf * HALF + AUG_POS_BLK + i] = part
    return tab


def _key_aug_table(seq):
    t = np.arange(seq)
    tab = np.zeros((seq, LANES), np.float32)
    for hf in range(2):
        tab[t, hf * HALF + AUG_SEL + t // MOBA_BLOCK] = 1.0
        tab[:, hf * HALF + AUG_POS_IN:hf * HALF + AUG_POS_IN + N_SPLIT] = (t % MOBA_BLOCK)[:, None]
        tab[:, hf * HALF + AUG_POS_BLK:hf * HALF + AUG_POS_BLK + N_SPLIT] = (t - t % MOBA_BLOCK)[:, None]
    return jnp.asarray(tab, BF16)


def _params(sem):
    return pltpu.CompilerParams(dimension_semantics=sem, vmem_limit_bytes=VMEM_LIMIT_BYTES)


def _rms(x, g):
    return x * lax.rsqrt(jnp.mean(x * x, axis=-1, keepdims=True) + EPS) * g


def _dot(a, b):
    return jnp.dot(a, b, preferred_element_type=F32)


def _dot_nt(a, b, precision=None):
    return lax.dot_general(a, b, (((1,), (1,)), ((), ())), preferred_element_type=F32,
                           precision=precision)


def _lambda(lp, lam_init):
    return (jnp.exp(jnp.sum(lp[0:1] * lp[1:2], axis=-1, keepdims=True))
            - jnp.exp(jnp.sum(lp[2:3] * lp[3:4], axis=-1, keepdims=True)) + lam_init)


def _proj_prompt_kernel(x_ref, g_ref, w_ref, atab_ref, dq_ref, dkt_ref, dkb_ref, dv_ref, dvb_ref,
                        mq_ref, mkt_ref, mkb_ref, mvt_ref, mvb_ref, aug_ref, kmt_ref, *, tm, n_blk_seq):
    xn = _rms(x_ref[...], g_ref[...]).astype(BF16)

    def col(c):
        return _dot(xn, w_ref[:, c * GROUP:(c + 1) * GROUP])

    dq_ref[...] = (col(0) * (DH_DIFF ** -0.5 * LOG2E)).astype(BF16)
    z = col(1)
    dkt_ref[...] = z.T
    dkb_ref[...] = z.astype(BF16)
    z = col(2)
    for h in range(H_DIFF):
        dv_ref[pl.ds(h, tm, stride=H_DIFF), :] = z[:, h * DV_DIFF:(h + 1) * DV_DIFF]
    dvb_ref[...] = z.astype(BF16)
    mq = col(3)
    mq_ref[...] = (mq * (DH_MOBA ** -0.5 * LOG2E)).astype(BF16)
    mk = col(4)
    mkt_ref[...] = mk.T
    mkb_ref[...] = mk.astype(BF16)
    z = col(5)
    mvt_ref[...] = z.T
    mvb_ref[...] = z.astype(BF16)

    t = pl.program_id(0)
    blk_per_tile = tm // MOBA_BLOCK
    tiles_per_seq = n_blk_seq // blk_per_tile
    blk0 = (t % tiles_per_seq) * blk_per_tile

    @pl.when(t == 0)
    def _():
        kmt_ref[...] = jnp.zeros_like(kmt_ref)

    lane_g = lax.broadcasted_iota(jnp.int32, (1, GROUP), 1) // DH_MOBA
    for r in range(blk_per_tile):
        km = jnp.mean(mk[r * MOBA_BLOCK:(r + 1) * MOBA_BLOCK], axis=0, keepdims=True)
        for h in range(H_MOBA):
            kmt_ref[pl.ds(h * SEL_STRIDE + blk0 + r, 1), :] = jnp.where(lane_g == h, km, 0.0)

    gate = _dot_nt(mq * (DH_MOBA ** -0.5), kmt_ref[...], precision=lax.Precision.HIGHEST)
    lane = lax.broadcasted_iota(jnp.int32, (tm, LANES), 1)
    row = lax.broadcasted_iota(jnp.int32, (tm, LANES), 0)
    n_in = lane % SEL_STRIDE
    own = blk0 + row // MOBA_BLOCK
    past = n_in < own
    g = jnp.where(past, gate, NEG_INF)
    cnt = jnp.zeros((tm, LANES), jnp.int32)
    for s in range(1, n_blk_seq):
        lo = pltpu.roll(g, s, 1)
        cnt += jnp.where((n_in >= s) & (lo >= g), 1, 0)
        hi = pltpu.roll(g, LANES - s, 1)
        cnt += jnp.where((n_in + s < SEL_STRIDE) & (hi > g), 1, 0)
    visible = (past & (cnt < MOBA_TOPK)) | (n_in == own)
    sel = jnp.where(visible, 0.0, NEG_INF).astype(BF16)
    er = lax.broadcasted_iota(jnp.int32, (LANES, GROUP), 0)
    ec = lax.broadcasted_iota(jnp.int32, (LANES, GROUP), 1)
    spread = jnp.where((ec // HALF == er // SEL_STRIDE) & (ec % HALF == AUG_SEL + er % SEL_STRIDE),
                       1.0, 0.0).astype(BF16)
    aug_ref[...] = (_dot(sel, spread) + atab_ref[...]).astype(BF16)


def _project_prompt(x2d, g, w_bf, aug_tab, *, tm, n_blk_seq):
    n, d = x2d.shape
    assert tm % MOBA_BLOCK == 0 and n_blk_seq % (tm // MOBA_BLOCK) == 0 and n_blk_seq <= SEL_STRIDE
    seq = n_blk_seq * MOBA_BLOCK
    tiles_per_seq = seq // tm
    row_spec = pl.BlockSpec((tm, GROUP), lambda t: (t, 0))
    t_spec = pl.BlockSpec((None, GROUP, tm), lambda t: (t // tiles_per_seq, 0, t % tiles_per_seq))
    f32_out = jax.ShapeDtypeStruct((n, GROUP), F32)
    bf_out = jax.ShapeDtypeStruct((n, GROUP), BF16)
    t_out = jax.ShapeDtypeStruct((n // seq, GROUP, seq), F32)
    return pl.pallas_call(
        functools.partial(_proj_prompt_kernel, tm=tm, n_blk_seq=n_blk_seq),
        grid=(n // tm,),
        in_specs=[pl.BlockSpec((tm, d), lambda t: (t, 0)),
                  pl.BlockSpec((1, d), lambda t: (0, 0)),
                  pl.BlockSpec((d, 6 * GROUP), lambda t: (0, 0)),
                  pl.BlockSpec((1, GROUP), lambda t: (0, 0))],
        out_specs=[row_spec, t_spec, row_spec, pl.BlockSpec((tm * H_DIFF, DV_DIFF), lambda t: (t, 0)),
                   row_spec, row_spec, t_spec, row_spec, t_spec, row_spec, row_spec],
        out_shape=[bf_out, t_out, bf_out, jax.ShapeDtypeStruct((n * H_DIFF, DV_DIFF), F32), bf_out, bf_out,
                   t_out, bf_out, t_out, bf_out, bf_out],
        scratch_shapes=[pltpu.VMEM((LANES, GROUP), F32)],
        compiler_params=_params(("arbitrary",)),
    )(x2d, g.reshape(1, d), w_bf, aug_tab)


def _proj_sample_kernel(x_ref, g_ref, w_ref, dq_ref, dk_ref, dv_ref, mq_ref, mk_ref, mv_ref):
    xn = _rms(x_ref[...], g_ref[...])
    scales = (DH_DIFF ** -0.5, 1.0, 1.0, DH_MOBA ** -0.5, 1.0, 1.0)
    for c, (ref, sc) in enumerate(zip((dq_ref, dk_ref, dv_ref, mq_ref, mk_ref, mv_ref), scales)):
        ref[...] = jnp.dot(xn, w_ref[:, c * GROUP:(c + 1) * GROUP], preferred_element_type=F32,
                           precision=lax.Precision.HIGHEST) * sc


def _project_sample(x2d, g, w_f32):
    n, d = x2d.shape
    out = jax.ShapeDtypeStruct((n, GROUP), F32)
    return pl.pallas_call(
        _proj_sample_kernel,
        grid=(1,),
        in_specs=[pl.BlockSpec((n, d), lambda t: (0, 0)),
                  pl.BlockSpec((1, d), lambda t: (0, 0)),
                  pl.BlockSpec((d, 6 * GROUP), lambda t: (0, 0))],
        out_specs=[pl.BlockSpec((n, GROUP), lambda t: (0, 0))] * 6,
        out_shape=[out] * 6,
        compiler_params=_params(("arbitrary",)),
    )(x2d, g.reshape(1, d), w_f32)


def _pair_attn_kernel(q_ref, aug_ref, k_ref, ktab_ref, v_ref, *rest, tq, moba, lam_init):
    if moba:
        o_ref, m_scr, acc_scr, s0_scr, s1_scr = rest
    else:
        lamp_ref, g_ref, o_ref, m_scr, acc_scr, s0_scr, s1_scr = rest
    i = pl.program_id(2)

    q = q_ref[...]
    if moba:
        aug = aug_ref[...]
    else:
        aug = jnp.broadcast_to(aug_ref[...], (tq, LANES)).astype(BF16)
    lane = lax.broadcasted_iota(jnp.int32, (tq, LANES), 1)
    lo = lane < HALF
    zero = jnp.zeros_like(q)
    q2 = jnp.concatenate(
        [jnp.concatenate([jnp.where(lo, q, zero), jnp.where(lo, aug, zero)], axis=1),
         jnp.concatenate([jnp.where(lo, zero, q), jnp.where(lo, zero, aug)], axis=1)], axis=0)

    m_scr[...] = jnp.full_like(m_scr, NEG_INF)
    acc_scr[...] = jnp.zeros_like(acc_scr)
    ones = jnp.ones((tq, LANES), BF16)
    n_rep = tq // LANES

    def scores(jb):
        start = pl.multiple_of(jb * tq, tq)
        kj = jnp.concatenate([k_ref[pl.ds(start, tq), :], ktab_ref[pl.ds(start, tq), :]], axis=1)
        return _dot_nt(q2, kj)

    def absorb(s, jb):
        start = pl.multiple_of(jb * tq, tq)
        vj = jnp.concatenate([v_ref[pl.ds(start, tq), :], ones], axis=1)
        m_old = m_scr[...]
        m_new = jnp.maximum(m_old, jnp.max(s, axis=-1, keepdims=True))
        alpha = jnp.exp2(m_old - m_new)
        pe = jnp.exp2(s - jnp.concatenate([m_new] * n_rep, axis=1))
        acc_scr[...] = jnp.concatenate([alpha, alpha], axis=1) * acc_scr[...] + _dot(pe.astype(BF16), vj)
        m_scr[...] = m_new

    r = lax.broadcasted_iota(jnp.int32, (2 * tq, tq), 0)
    c = lax.broadcasted_iota(jnp.int32, (2 * tq, tq), 1)
    s0_scr[...] = jnp.where(r % tq >= c, scores(i), NEG_INF)
    last = jnp.maximum(i - 1, 0)

    def body(u, carry):
        s1_scr[...] = scores(jnp.minimum(2 * u, last))
        absorb(s0_scr[...], jnp.where(u == 0, i, 2 * u - 1))

        @pl.when(2 * u + 1 <= i)
        def _():
            s0_scr[...] = scores(jnp.minimum(2 * u + 1, last))
            absorb(s1_scr[...], 2 * u)

        return carry

    lax.fori_loop(0, i // 2 + 1, body, 0)

    acc = acc_scr[...]
    o2 = acc[:, :LANES] / acc[:, LANES:]
    o0, o1 = o2[:tq], o2[tq:]
    if moba:
        o_ref[...] = jnp.where(lo, o0, o1).astype(o_ref.dtype)
    else:
        o = o0 - _lambda(lamp_ref[...], lam_init) * o1
        o_ref[...] = (_rms(o, g_ref[...]) * (1.0 - lam_init)).astype(o_ref.dtype)


def _pair_attention(q_bf, aug, k_bf, ktab, v_bf, *, nb, seq, tq, moba, lamp=None, g_subln=None,
                    lam_init=0.0):
    n_pair = GROUP // LANES
    nq = seq // tq
    assert tq % MOBA_BLOCK == 0 and seq % tq == 0
    q_spec = pl.BlockSpec((tq, LANES), lambda b, p, i: (b * nq + i, p))
    kv_spec = pl.BlockSpec((seq, LANES), lambda b, p, i: (b, p))
    if moba:
        aug_spec = q_spec
    else:
        aug_spec = pl.BlockSpec((None, 1, LANES), lambda b, p, i: (p, 0, 0))
    in_specs = [q_spec, aug_spec, kv_spec, pl.BlockSpec((seq, LANES), lambda b, p, i: (0, 0)), kv_spec]
    args = [q_bf, aug, k_bf, ktab, v_bf]
    if not moba:
        in_specs += [pl.BlockSpec((4, DH_DIFF), lambda b, p, i: (0, 0)),
                     pl.BlockSpec((1, DV_DIFF), lambda b, p, i: (0, 0))]
        args += [lamp, g_subln.reshape(1, DV_DIFF)]
    return pl.pallas_call(
        functools.partial(_pair_attn_kernel, tq=tq, moba=moba, lam_init=lam_init),
        grid=(nb, n_pair, nq),
        in_specs=in_specs,
        out_specs=q_spec,
        out_shape=jax.ShapeDtypeStruct((nb * seq, GROUP), BF16),
        scratch_shapes=[pltpu.VMEM((2 * tq, LANES), F32), pltpu.VMEM((2 * tq, 2 * LANES), F32),
                        pltpu.VMEM((2 * tq, tq), F32), pltpu.VMEM((2 * tq, tq), F32)],
        compiler_params=_params(("arbitrary", "arbitrary", "arbitrary")),
    )(*args)


def _norm_mm_kernel(x_ref, g_ref, w_ref, o_ref, *, scale, relu2):
    y = _dot(_rms(x_ref[...], g_ref[...]).astype(BF16), w_ref[...])
    if relu2:
        y = jnp.square(jnp.maximum(y, 0.0))
    if scale != 1.0:
        y = y * scale
    o_ref[...] = y.astype(o_ref.dtype)


def _norm_matmul(x2d, g, w_bf, *, tm, out_dtype, scale=1.0, relu2=False):
    n, d = x2d.shape
    e = w_bf.shape[1]
    return pl.pallas_call(
        functools.partial(_norm_mm_kernel, scale=scale, relu2=relu2),
        grid=(n // tm,),
        in_specs=[pl.BlockSpec((tm, d), lambda t: (t, 0)),
                  pl.BlockSpec((1, d), lambda t: (0, 0)),
                  pl.BlockSpec((d, e), lambda t: (0, 0))],
        out_specs=pl.BlockSpec((tm, e), lambda t: (t, 0)),
        out_shape=jax.ShapeDtypeStruct((n, e), out_dtype),
        compiler_params=_params(("arbitrary",)),
    )(x2d, g.reshape(1, d), w_bf)


def _mm_norm_res_kernel(*refs, n_in):
    a_refs = refs[:n_in]
    w_refs = refs[n_in:2 * n_in]
    x_ref, g_ref, o_ref = refs[2 * n_in:]
    y = _dot(a_refs[0][...].astype(BF16), w_refs[0][...])
    for a_ref, w_ref in zip(a_refs[1:], w_refs[1:]):
        y = y + _dot(a_ref[...].astype(BF16), w_ref[...])
    o_ref[...] = x_ref[...] + _rms(y, g_ref[...])


def _matmul_norm_residual(a_list, w_list, x2d, g, *, tm):
    n, d = x2d.shape
    n_in = len(a_list)
    in_specs = [pl.BlockSpec((tm, a.shape[1]), lambda t: (t, 0)) for a in a_list]
    in_specs += [pl.BlockSpec(w.shape, lambda t: (0, 0)) for w in w_list]
    in_specs += [pl.BlockSpec((tm, d), lambda t: (t, 0)), pl.BlockSpec((1, d), lambda t: (0, 0))]
    return pl.pallas_call(
        functools.partial(_mm_norm_res_kernel, n_in=n_in),
        grid=(n // tm,),
        in_specs=in_specs,
        out_specs=pl.BlockSpec((tm, d), lambda t: (t, 0)),
        out_shape=jax.ShapeDtypeStruct((n, d), F32),
        compiler_params=_params(("arbitrary",)),
    )(*a_list, *w_list, x2d, g.reshape(1, d))


def _mem_attn_kernel(q_ref, k_ref, v_ref, o_ref, *, interleaved):
    q = q_ref[...]
    outs = []
    for h in range(H_MEM):
        cols = slice(h * DH_MEM, (h + 1) * DH_MEM)
        if interleaved:
            kh = k_ref[pl.ds(h, N_MEM, stride=H_MEM), :]
            vh = v_ref[pl.ds(h, N_MEM, stride=H_MEM), :]
        else:
            kh, vh = k_ref[:, cols], v_ref[:, cols]
        s = _dot_nt(q[:, cols], kh.astype(BF16))
        pe = jnp.exp(s - jnp.max(s, axis=-1, keepdims=True))
        o = _dot(pe.astype(BF16), vh.astype(BF16))
        outs.append(o / jnp.sum(pe, axis=-1, keepdims=True))
    o_ref[...] = jnp.concatenate(outs, axis=1).astype(o_ref.dtype)


def _mem_attention(q_bf, mem_k, mem_v, *, nb, rows, tm):
    d = H_MEM * DH_MEM
    nt = rows // tm
    interleaved = mem_k.shape[1] == N_MEM * H_MEM
    mem_block = (None,) + mem_k.shape[1:]
    return pl.pallas_call(
        functools.partial(_mem_attn_kernel, interleaved=interleaved),
        grid=(nb, nt),
        in_specs=[pl.BlockSpec((tm, d), lambda b, t: (b * nt + t, 0)),
                  pl.BlockSpec(mem_block, lambda b, t: (b, 0, 0)),
                  pl.BlockSpec(mem_block, lambda b, t: (b, 0, 0))],
        out_specs=pl.BlockSpec((tm, d), lambda b, t: (b * nt + t, 0)),
        out_shape=jax.ShapeDtypeStruct((nb * rows, d), BF16),
        compiler_params=_params(("arbitrary", "arbitrary")),
    )(q_bf, mem_k, mem_v)


def _mlp_kernel(x_ref, g_pre_ref, w_up_ref, w_down_ref, g_post_ref, o_ref):
    x = x_ref[...]
    h = _dot(_rms(x, g_pre_ref[...]).astype(BF16), w_up_ref[...])
    h = jnp.square(jnp.maximum(h, 0.0)).astype(BF16)
    o_ref[...] = x + _rms(_dot(h, w_down_ref[...]), g_post_ref[...])


def _mlp(x2d, g_pre, w_up, w_down, g_post, *, tm):
    n, d = x2d.shape
    f = w_up.shape[1]
    const = lambda t: (0, 0)
    resident = pl.Buffered(1)
    return pl.pallas_call(
        _mlp_kernel,
        grid=(n // tm,),
        in_specs=[pl.BlockSpec((tm, d), lambda t: (t, 0)),
                  pl.BlockSpec((1, d), const),
                  pl.BlockSpec((d, f), const, pipeline_mode=resident),
                  pl.BlockSpec((f, d), const, pipeline_mode=resident),
                  pl.BlockSpec((1, d), const)],
        out_specs=pl.BlockSpec((tm, d), lambda t: (t, 0)),
        out_shape=jax.ShapeDtypeStruct((n, d), F32),
        compiler_params=_params(("arbitrary",)),
    )(x2d, g_pre.reshape(1, d), w_up, w_down, g_post.reshape(1, d))


def _finish_layer(x2d, o_diff, o_moba, mem_k, mem_v, wts, *, nb, rows, tm, rows_pad=1):
    x1 = _matmul_norm_residual([o_diff, o_moba], [wts['w_out_a'], wts['w_out_b']], x2d,
                               wts['g_post_mix'], tm=tm)
    q = _norm_matmul(x1, wts['g_pre_x'], wts['w_xq'], tm=tm, out_dtype=BF16, scale=DH_MEM ** -0.5)
    if rows_pad > 1:
        qp = jnp.broadcast_to(q[:, None, :], (nb * rows, rows_pad, q.shape[1])).reshape(-1, q.shape[1])
        c = _mem_attention(qp, mem_k, mem_v, nb=nb, rows=rows * rows_pad, tm=rows * rows_pad)
        c = c.reshape(nb * rows, rows_pad, -1)[:, 0, :]
    else:
        c = _mem_attention(q, mem_k, mem_v, nb=nb, rows=rows, tm=min(tm, rows))
    x2 = _matmul_norm_residual([c], [wts['w_xo']], x1, wts['g_post_x'], tm=tm)
    return _mlp(x2, wts['g_pre_ff'], wts['w_up'], wts['w_down'], wts['g_post_ff'], tm=tm)


def _query_columns(q_row):
    r = lax.broadcasted_iota(jnp.int32, (GROUP, GROUP), 0)
    c = lax.broadcasted_iota(jnp.int32, (GROUP, GROUP), 1)
    col = jnp.sum(jnp.where(r == c, jnp.broadcast_to(q_row, (GROUP, GROUP)), 0.0), axis=-1, keepdims=True)
    return jnp.broadcast_to(col, (GROUP, LANES))


def _page_scores(kt, qb, n_grp):
    prod = kt * qb
    w = GROUP // n_grp
    return jnp.concatenate([jnp.sum(prod[g * w:(g + 1) * w], axis=0, keepdims=True) for g in range(n_grp)],
                           axis=0)


def _self_scores(q_row, k_row, n_grp):
    row = lax.broadcasted_iota(jnp.int32, (n_grp, GROUP), 0)
    lane = lax.broadcasted_iota(jnp.int32, (n_grp, GROUP), 1)
    prod = jnp.broadcast_to(q_row * k_row, (n_grp, GROUP))
    return jnp.sum(jnp.where(lane // (GROUP // n_grp) == row, prod, 0.0), axis=-1, keepdims=True)


def _diff_decode_kernel(pt_ref, q_ref, ks_ref, vs_ref, lamp_ref, g_ref, *rest, n_pg, past, lam_init):
    k_refs = rest[:n_pg]
    v_refs = rest[n_pg:2 * n_pg]
    o_ref, qb_scr, m_scr, l_scr, acc_scr = rest[2 * n_pg:]
    step = pl.program_id(1)
    n_rows = 2 * H_DIFF
    page = k_refs[0].shape[1]
    slopes = _alibi_slopes(H_DIFF)
    row1 = lax.broadcasted_iota(jnp.int32, (n_rows, 1), 0)
    slope_col = jnp.zeros((n_rows, 1), F32)
    for h in range(H_DIFF):
        slope_col = jnp.where(row1 // 2 == h, float(slopes[h]), slope_col)

    @pl.when(step == 0)
    def _():
        qb_scr[...] = _query_columns(q_ref[...])
        m_scr[...] = jnp.broadcast_to(_self_scores(q_ref[...], ks_ref[...], n_rows), (n_rows, LANES))
        l_scr[...] = jnp.ones_like(l_scr)
        for h in range(H_DIFF):
            acc_scr[h] = jnp.broadcast_to(vs_ref[:, h * DV_DIFF:(h + 1) * DV_DIFF], (n_rows, DV_DIFF))

    qb = qb_scr[...]
    s = jnp.concatenate([_page_scores(k_refs[pg][...], qb, n_rows) for pg in range(n_pg)], axis=1)
    kpos = step * (n_pg * page) + lax.broadcasted_iota(jnp.int32, (1, n_pg * page), 1)
    s = s - slope_col * (past - kpos).astype(F32)
    m_old = m_scr[...]
    m_new = jnp.maximum(m_old, jnp.max(s, axis=-1, keepdims=True))
    alpha = jnp.exp(m_old - m_new)
    pe = jnp.exp(s - jnp.concatenate([m_new] * n_pg, axis=1))
    l_scr[...] = alpha * l_scr[...] + jnp.sum(pe, axis=-1, keepdims=True)
    pe = pe.astype(BF16)
    for h in range(H_DIFF):
        pv = None
        for pg in range(n_pg):
            vh = v_refs[pg][pl.ds(h, page, stride=H_DIFF), :].astype(BF16)
            d = _dot(pe[:, pg * page:(pg + 1) * page], vh)
            pv = d if pv is None else pv + d
        acc_scr[h] = alpha * acc_scr[h] + pv
    m_scr[...] = m_new

    @pl.when(step == pl.num_programs(1) - 1)
    def _():
        lam = _lambda(lamp_ref[...], lam_init)
        row = lax.broadcasted_iota(jnp.int32, (n_rows, DV_DIFF), 0)
        inv_l = 1.0 / l_scr[...]
        parts = []
        for h in range(H_DIFF):
            coef = jnp.where(row == 2 * h, 1.0, jnp.where(row == 2 * h + 1, -lam, 0.0))
            o = jnp.sum(acc_scr[h] * inv_l * coef, axis=0, keepdims=True)
            parts.append(_rms(o, g_ref[...]))
        o_ref[...] = jnp.concatenate(parts, axis=1) * (1.0 - lam_init)


def _diff_decode(page_table, q, k_self, v_self, lamp, g_subln, cache_kt, cache_v, *,
                 layer_off, n_pg, lam_init):
    nb, n_pages = page_table.shape
    page = cache_kt.shape[2]
    past = n_pages * page
    assert n_pages % n_pg == 0 and page == LANES

    def page_spec(pg, rows, cols):
        return pl.BlockSpec((None, rows, cols),
                            lambda b, s, pt: (layer_off + pt[b * n_pages + s * n_pg + pg], 0, 0))

    vec_spec = pl.BlockSpec((None, 1, GROUP), lambda b, s, pt: (b, 0, 0))
    n_rows = 2 * H_DIFF
    grid_spec = pltpu.PrefetchScalarGridSpec(
        num_scalar_prefetch=1,
        grid=(nb, n_pages // n_pg),
        in_specs=[vec_spec, vec_spec, vec_spec,
                  pl.BlockSpec((4, DH_DIFF), lambda b, s, pt: (0, 0)),
                  pl.BlockSpec((1, DV_DIFF), lambda b, s, pt: (0, 0))]
                 + [page_spec(pg, GROUP, page) for pg in range(n_pg)]
                 + [page_spec(pg, page * H_DIFF, DV_DIFF) for pg in range(n_pg)],
        out_specs=vec_spec,
        scratch_shapes=[pltpu.VMEM((GROUP, LANES), F32), pltpu.VMEM((n_rows, LANES), F32),
                        pltpu.VMEM((n_rows, LANES), F32), pltpu.VMEM((H_DIFF, n_rows, DV_DIFF), F32)],
    )
    out = pl.pallas_call(
        functools.partial(_diff_decode_kernel, n_pg=n_pg, past=past, lam_init=lam_init),
        grid_spec=grid_spec,
        out_shape=jax.ShapeDtypeStruct((nb, 1, GROUP), F32),
        compiler_params=_params(("arbitrary", "arbitrary")),
    )(page_table.reshape(-1), q.reshape(nb, 1, GROUP), k_self.reshape(nb, 1, GROUP),
      v_self.reshape(nb, 1, GROUP), lamp, g_subln.reshape(1, DV_DIFF),
      *([cache_kt] * n_pg), *([cache_v] * n_pg))
    return out.reshape(nb, GROUP)


def _moba_scores_kernel(pt_ref, q_ref, ks_ref, *rest, n_pg, n_pages, past):
    k_refs = rest[:n_pg]
    p_ref, sel_ref, pself_ref, qb_scr, s_scr = rest[n_pg:]
    step = pl.program_id(1)
    page = k_refs[0].shape[1]

    @pl.when(step == 0)
    def _():
        qb_scr[...] = _query_columns(q_ref[...])

    qb = qb_scr[...]
    for pg in range(n_pg):
        s_scr[step * n_pg + pg] = _page_scores(k_refs[pg][...], qb, H_MOBA)

    @pl.when(step == pl.num_programs(1) - 1)
    def _():
        slopes = _alibi_slopes(H_MOBA)
        s_self = _self_scores(q_ref[...], ks_ref[...], H_MOBA)
        pg_i = lax.broadcasted_iota(jnp.int32, (n_pages, page), 0)
        pg_f = pg_i.astype(F32)
        blk_f = jnp.floor(pg_f * 0.5)
        dist = (past - (pg_i * page + lax.broadcasted_iota(jnp.int32, (n_pages, page), 1))).astype(F32)
        row = lax.broadcasted_iota(jnp.int32, (H_MOBA, LANES), 0)
        lane = lax.broadcasted_iota(jnp.int32, (H_MOBA, LANES), 1)
        sel_out = jnp.zeros((H_MOBA, LANES), F32)
        pself_out = jnp.zeros((H_MOBA, LANES), F32)
        for h in range(H_MOBA):
            sh = s_scr[:, h, :]
            rs = jnp.broadcast_to(jnp.sum(sh, axis=-1, keepdims=True), (n_pages, page))
            other = jnp.where(pg_i % 2 == 0, pltpu.roll(rs, n_pages - 1, 0), pltpu.roll(rs, 1, 0))
            g = (rs + other) * (1.0 / MOBA_BLOCK)
            chosen = jnp.zeros((n_pages, page), jnp.bool_)
            for r in range(MOBA_TOPK):
                mx = jnp.max(g, axis=0, keepdims=True)
                first = jnp.min(jnp.where(g == mx, pg_f, float(n_pages)), axis=0, keepdims=True)
                idx = jnp.floor(first * 0.5)
                pick = blk_f == idx
                chosen = chosen | pick
                g = jnp.where(pick, REMOVED, g)
                sel_out = jnp.where((row == h) & (lane == r), idx, sel_out)
            logit = jnp.where(chosen, sh - float(slopes[h]) * dist, NEG_INF)
            ss = s_self[h:h + 1]
            m = jnp.maximum(jnp.max(jnp.max(logit, axis=-1, keepdims=True), axis=0, keepdims=True), ss)
            pe = jnp.exp(logit - m)
            ps = jnp.exp(ss - m)
            inv = 1.0 / (jnp.sum(jnp.sum(pe, axis=-1, keepdims=True), axis=0, keepdims=True) + ps)
            p_ref[h] = pe * inv
            pself_out = jnp.where(row == h, ps * inv, pself_out)
        sel_ref[...] = sel_out.astype(jnp.int32)
        pself_ref[...] = pself_out


def _moba_scores(page_table, q, k_self, cache_kt, *, layer_off, n_pg):
    nb, n_pages = page_table.shape
    page = cache_kt.shape[2]
    past = n_pages * page
    n_blk = past // MOBA_BLOCK
    pg_per_blk = MOBA_BLOCK // page
    assert n_blk >= MOBA_TOPK and n_pages % n_pg == 0 and pg_per_blk == 2 and page == LANES

    def page_spec(pg):
        return pl.BlockSpec((None, GROUP, page),
                            lambda b, s, pt: (layer_off + pt[b * n_pages + s * n_pg + pg], 0, 0))

    vec_spec = pl.BlockSpec((None, 1, GROUP), lambda b, s, pt: (b, 0, 0))
    small_spec = pl.BlockSpec((None, H_MOBA, LANES), lambda b, s, pt: (b, 0, 0))
    grid_spec = pltpu.PrefetchScalarGridSpec(
        num_scalar_prefetch=1,
        grid=(nb, n_pages // n_pg),
        in_specs=[vec_spec, vec_spec] + [page_spec(pg) for pg in range(n_pg)],
        out_specs=[pl.BlockSpec((None, H_MOBA, n_pages, page), lambda b, s, pt: (b, 0, 0, 0)),
                   small_spec, small_spec],
        scratch_shapes=[pltpu.VMEM((GROUP, LANES), F32), pltpu.VMEM((n_pages, H_MOBA, page), F32)],
    )
    return pl.pallas_call(
        functools.partial(_moba_scores_kernel, n_pg=n_pg, n_pages=n_pages, past=past),
        grid_spec=grid_spec,
        out_shape=[jax.ShapeDtypeStruct((nb, H_MOBA, n_pages, page), F32),
                   jax.ShapeDtypeStruct((nb, H_MOBA, LANES), jnp.int32),
                   jax.ShapeDtypeStruct((nb, H_MOBA, LANES), F32)],
        compiler_params=_params(("arbitrary", "arbitrary")),
    )(page_table.reshape(-1), q.reshape(nb, 1, GROUP), k_self.reshape(nb, 1, GROUP),
      *([cache_kt] * n_pg))


def _moba_values_kernel(pt_ref, sel_ref, p_ref, pself_ref, vs_ref, *rest, n_chunk):
    v_refs = rest[:n_chunk]
    o_ref = rest[n_chunk]
    b = pl.program_id(0)
    h0 = pl.program_id(1) * (2 * VALUE_PAIRS)
    page = v_refs[0].shape[1]
    pg_per_blk = MOBA_BLOCK // page
    lane = lax.broadcasted_iota(jnp.int32, (SUBLANES, LANES), 1)
    outs = []
    for pp_i in range(VALUE_PAIRS):
        pair = jnp.zeros((SUBLANES, LANES), F32)
        for hf in range(2):
            hl = 2 * pp_i + hf
            h = h0 + hl
            acc = pself_ref[pl.ds(h, 1), :] * vs_ref[:, pp_i * LANES:(pp_i + 1) * LANES]
            acc = jnp.broadcast_to(acc, (SUBLANES, LANES))
            for r in range(MOBA_TOPK):
                blk = sel_ref[(b * H_MOBA + h) * MOBA_TOPK + r]
                for pg in range(pg_per_blk):
                    pp = jnp.broadcast_to(p_ref[hl, pl.ds(blk * pg_per_blk + pg, 1), :], (SUBLANES, page))
                    vt = v_refs[(hl * MOBA_TOPK + r) * pg_per_blk + pg][...]
                    acc = acc + _dot_nt(pp.astype(BF16), vt.astype(BF16))
            pair = jnp.where(lane // HALF == hf, acc, pair)
        outs.append(pair[0:1])
    o_ref[...] = jnp.concatenate(outs, axis=1)


def _moba_values(page_table, sel, p, pself, v_self, cache_vt, *, layer_off):
    nb, n_pages = page_table.shape
    page = cache_vt.shape[2]
    pg_per_blk = MOBA_BLOCK // page
    heads_per_step = 2 * VALUE_PAIRS
    n_step = H_MOBA // heads_per_step
    n_chunk = heads_per_step * MOBA_TOPK * pg_per_blk

    def chunk_spec(hl, r, pg):
        def index_map(b, st, pt, sl):
            blk = sl[(b * H_MOBA + st * heads_per_step + hl) * MOBA_TOPK + r]
            return (layer_off + pt[b * n_pages + blk * pg_per_blk + pg], st * VALUE_PAIRS + hl // 2, 0)
        return pl.BlockSpec((None, LANES, page), index_map)

    pair_vec = pl.BlockSpec((None, 1, VALUE_PAIRS * LANES), lambda b, st, pt, sl: (b, 0, st))
    grid_spec = pltpu.PrefetchScalarGridSpec(
        num_scalar_prefetch=2,
        grid=(nb, n_step),
        in_specs=[pl.BlockSpec((None, heads_per_step, n_pages, page), lambda b, st, pt, sl: (b, st, 0, 0)),
                  pl.BlockSpec((None, H_MOBA, LANES), lambda b, st, pt, sl: (b, 0, 0)),
                  pair_vec]
                 + [chunk_spec(hl, r, pg) for hl in range(heads_per_step) for r in range(MOBA_TOPK)
                    for pg in range(pg_per_blk)],
        out_specs=pair_vec,
    )
    out = pl.pallas_call(
        functools.partial(_moba_values_kernel, n_chunk=n_chunk),
        grid_spec=grid_spec,
        out_shape=jax.ShapeDtypeStruct((nb, 1, GROUP), F32),
        compiler_params=_params(("arbitrary", "arbitrary")),
    )(page_table.reshape(-1), sel.reshape(-1), p, pself, v_self.reshape(nb, 1, GROUP),
      *([cache_vt] * n_chunk))
    return out.reshape(nb, GROUP)


PROMPT_TILE = 512
ATTN_TILE = 512
DIFF_DECODE_PAGES = 32
MOBA_DECODE_PAGES = 64
VALUE_PAIRS = 2


def _feature_major(cache, n_lead):
    nd = cache.ndim
    perm = (0, 1) + tuple(range(3, nd)) + (2,)
    t = jnp.transpose(cache, perm)
    return t.reshape(n_lead, -1, cache.shape[2])


def kernel(x_prompt, x_sample, cache_diff_k, cache_diff_v, cache_moba_k, cache_moba_v, cache_mem_k, cache_mem_v, page_table, mem_prompt, g_pre_mix, w_in, lambda_q1, lambda_k1, lambda_q2, lambda_k2, g_subln, w_out, g_post_mix, g_mem, w_mem_k, w_mem_v, g_pre_x, w_xq, w_xo, g_post_x, g_pre_ff, w_up, w_down, g_post_ff):
    nb_p, seq, d = x_prompt.shape
    nb_s, s_len, _ = x_sample.shape
    assert s_len == 1, "the sample kernels handle one new token per sequence"
    depth = w_in.shape[0]
    n_pool, page = cache_diff_k.shape[1], cache_diff_k.shape[2]
    assert MOBA_BLOCK % page == 0 and (page_table.shape[1] * page) % MOBA_BLOCK == 0
    n_blk_seq = seq // MOBA_BLOCK
    tm = min(PROMPT_TILE, seq)
    tq = min(ATTN_TILE, seq)

    xp = x_prompt.reshape(nb_p * seq, d)
    xs = x_sample.reshape(nb_s, d)
    ckt_d = _feature_major(cache_diff_k, depth * n_pool)
    cv_d = cache_diff_v.reshape(depth * n_pool, page * H_DIFF, DV_DIFF)
    ckt_m = _feature_major(cache_moba_k, depth * n_pool)
    cvt_m = _feature_major(cache_moba_v, depth * n_pool)
    mem_x = mem_prompt.reshape(nb_p * N_MEM, d)
    slopes_d = _alibi_slopes(H_DIFF)
    slopes_m = _alibi_slopes(H_MOBA)
    ktab = _key_aug_table(seq)
    aug_d = jnp.asarray(_query_aug_table(np.stack([slopes_d, slopes_d], axis=1)))
    aug_m = jnp.asarray(_query_aug_table(slopes_m.reshape(-1, 2)).reshape(1, GROUP))

    outs = [[] for _ in range(10)]
    for l in range(depth):
        lam_init = 0.8 - 0.6 * math.exp(-0.3 * l)
        lamp = jnp.stack([lambda_q1[l], lambda_k1[l], lambda_q2[l], lambda_k2[l]]).astype(F32)
        w_in_bf = w_in[l].astype(BF16)
        w_out_bf = w_out[l].astype(BF16)
        wts = {
            'w_out_a': w_out_bf[:GROUP], 'w_out_b': w_out_bf[GROUP:], 'g_post_mix': g_post_mix[l],
            'g_pre_x': g_pre_x[l], 'w_xq': w_xq[l].astype(BF16), 'w_xo': w_xo[l].astype(BF16),
            'g_post_x': g_post_x[l], 'g_pre_ff': g_pre_ff[l], 'w_up': w_up[l].astype(BF16),
            'w_down': w_down[l].astype(BF16), 'g_post_ff': g_post_ff[l],
        }

        (dq, dk_t, dk_bf, dv, dv_bf, mq, mk_t, mk_bf, mv_t, mv_bf, mq_aug) = _project_prompt(
            xp, g_pre_mix[l], w_in_bf, aug_m, tm=tm, n_blk_seq=n_blk_seq)
        o_d = _pair_attention(dq, aug_d, dk_bf, ktab, dv_bf, nb=nb_p, seq=seq, tq=tq, moba=False,
                              lamp=lamp, g_subln=g_subln[l], lam_init=lam_init)
        o_m = _pair_attention(mq, mq_aug, mk_bf, ktab, mv_bf, nb=nb_p, seq=seq, tq=tq, moba=True)
        mem_tm = min(PROMPT_TILE, nb_p * N_MEM)
        memk = _norm_matmul(mem_x, g_mem[l], w_mem_k[l].astype(BF16), tm=mem_tm, out_dtype=F32)
        memv = _norm_matmul(mem_x, g_mem[l], w_mem_v[l].astype(BF16), tm=mem_tm, out_dtype=F32)
        xp = _finish_layer(xp, o_d, o_m, memk.reshape(nb_p, N_MEM, -1), memv.reshape(nb_p, N_MEM, -1),
                           wts, nb=nb_p, rows=seq, tm=tm)
        outs[0].append(jnp.transpose(dk_t.reshape(nb_p, H_DIFF, 2, DH_DIFF, seq), (0, 4, 1, 2, 3)))
        outs[1].append(dv.reshape(nb_p, seq, H_DIFF, DV_DIFF))
        outs[2].append(jnp.transpose(mk_t.reshape(nb_p, H_MOBA, DH_MOBA, seq), (0, 3, 1, 2)))
        outs[3].append(jnp.transpose(mv_t.reshape(nb_p, H_MOBA, DH_MOBA, seq), (0, 3, 1, 2)))
        outs[4].append(memk.reshape(nb_p, N_MEM, H_MEM, DH_MEM))
        outs[5].append(memv.reshape(nb_p, N_MEM, H_MEM, DH_MEM))

        sq, sk, sv, tq_s, tk, tv = _project_sample(xs, g_pre_mix[l], w_in[l])
        o_ds = _diff_decode(page_table, sq, sk, sv, lamp, g_subln[l], ckt_d, cv_d,
                            layer_off=l * n_pool, n_pg=math.gcd(DIFF_DECODE_PAGES, page_table.shape[1]),
                            lam_init=lam_init)
        p_s, sel_s, pself_s = _moba_scores(page_table, tq_s, tk, ckt_m, layer_off=l * n_pool,
                                           n_pg=math.gcd(MOBA_DECODE_PAGES, page_table.shape[1]))
        o_ms = _moba_values(page_table, sel_s[:, :, :MOBA_TOPK], p_s, pself_s, tv, cvt_m,
                            layer_off=l * n_pool)
        xs = _finish_layer(xs, o_ds, o_ms, cache_mem_k[l].reshape(nb_s, N_MEM * H_MEM, DH_MEM),
                           cache_mem_v[l].reshape(nb_s, N_MEM * H_MEM, DH_MEM), wts, nb=nb_s, rows=1, tm=nb_s,
                           rows_pad=SUBLANES)
        outs[6].append(sk.reshape(nb_s, 1, H_DIFF, 2, DH_DIFF))
        outs[7].append(sv.reshape(nb_s, 1, H_DIFF, DV_DIFF))
        outs[8].append(tk.reshape(nb_s, 1, H_MOBA, DH_MOBA))
        outs[9].append(tv.reshape(nb_s, 1, H_MOBA, DH_MOBA))

    return (xp.reshape(nb_p, seq, d), xs.reshape(nb_s, 1, d), *[jnp.stack(o) for o in outs])
```

```python
import functools
import math

import numpy as np
import jax
import jax.numpy as jnp
from jax import lax
from jax.experimental import pallas as pl
from jax.experimental.pallas import tpu as pltpu

F32 = jnp.float32
BF16 = jnp.bfloat16

EPS = 1e-6
NEG_INF = -1e30
REMOVED = -3e38
LOG2E = 1.4426950408889634

H_DIFF = 4
DH_DIFF = 64
DV_DIFF = 2 * DH_DIFF
H_MOBA = 8
DH_MOBA = 64
MOBA_BLOCK = 256
MOBA_TOPK = 3
H_MEM = 4
DH_MEM = 128
N_MEM = 256
GROUP = 512
LANES = 128
SUBLANES = 8
HALF = LANES // 2
SEL_STRIDE = LANES // H_MOBA

AUG_SEL = 0
AUG_POS_IN = 16
AUG_POS_BLK = 19
N_SPLIT = 3

VMEM_LIMIT_BYTES = 56 * 1024 * 1024


def _alibi_slopes(n):
    return np.power(2.0, -8.0 * np.arange(1, n + 1) / n).astype(np.float32)


def _split_bf16(x):
    parts, rest = [], np.asarray(x, np.float32)
    for _ in range(N_SPLIT):
        p = rest.astype(BF16).astype(np.float32)
        parts.append(p)
        rest = rest - p
    return parts


def _query_aug_table(slopes_per_half):
    slopes_per_half = np.asarray(slopes_per_half, np.float32)
    tab = np.zeros((slopes_per_half.shape[0], 1, LANES), np.float32)
    for hf in range(2):
        for i, part in enumerate(_split_bf16(slopes_per_half[:, hf] * np.float32(LOG2E))):
            tab[:, 0, hf * HALF + AUG_POS_IN + i] = part
            tab[:, 0, hf * HALF + AUG_POS_BLK + i] = part
    return tab


def _key_aug_table(seq):
    t = np.arange(seq)
    tab = np.zeros((seq, LANES), np.float32)
    for hf in range(2):
        tab[t, hf * HALF + AUG_SEL + t // MOBA_BLOCK] = 1.0
        tab[:, hf * HALF + AUG_POS_IN:hf * HALF + AUG_POS_IN + N_SPLIT] = (t % MOBA_BLOCK)[:, None]
        tab[:, hf * HALF + AUG_POS_BLK:hf * HALF + AUG_POS_BLK + N_SPLIT] = (t - t % MOBA_BLOCK)[:, None]
    return jnp.asarray(tab, BF16)


def _params(sem):
    return pltpu.CompilerParams(dimension_semantics=sem, vmem_limit_bytes=VMEM_LIMIT_BYTES)


def _rms(x, g):
    return x * lax.rsqrt(jnp.mean(x * x, axis=-1, keepdims=True) + EPS) * g


def _dot(a, b):
    return jnp.dot(a, b, preferred_element_type=F32)


def _dot_nt(a, b, precision=None):
    return lax.dot_general(a, b, (((1,), (1,)), ((), ())), preferred_element_type=F32,
                           precision=precision)


def _lambda(lp, lam_init):
    return (jnp.exp(jnp.sum(lp[0:1] * lp[1:2], axis=-1, keepdims=True))
            - jnp.exp(jnp.sum(lp[2:3] * lp[3:4], axis=-1, keepdims=True)) + lam_init)


def _proj_prompt_kernel(x_ref, g_ref, w_ref, atab_ref, dq_ref, dkt_ref, dkb_ref, dv_ref, dvb_ref,
                        mq_ref, mkt_ref, mkb_ref, mvt_ref, mvb_ref, aug_ref, kmt_ref, *, tm, n_blk_seq):
    xn = _rms(x_ref[...], g_ref[...]).astype(BF16)

    def col(c):
        return _dot(xn, w_ref[:, c * GROUP:(c + 1) * GROUP])

    dq_ref[...] = (col(0) * (DH_DIFF ** -0.5 * LOG2E)).astype(BF16)
    z = col(1)
    dkt_ref[...] = z.T
    dkb_ref[...] = z.astype(BF16)
    z = col(2)
    for h in range(H_DIFF):
        dv_ref[pl.ds(h, tm, stride=H_DIFF), :] = z[:, h * DV_DIFF:(h + 1) * DV_DIFF]
    dvb_ref[...] = z.astype(BF16)
    mq = col(3)
    mq_ref[...] = (mq * (DH_MOBA ** -0.5 * LOG2E)).astype(BF16)
    mk = col(4)
    mkt_ref[...] = mk.T
    mkb_ref[...] = mk.astype(BF16)
    z = col(5)
    mvt_ref[...] = z.T
    mvb_ref[...] = z.astype(BF16)

    t = pl.program_id(0)
    blk_per_tile = tm // MOBA_BLOCK
    tiles_per_seq = n_blk_seq // blk_per_tile
    blk0 = (t % tiles_per_seq) * blk_per_tile

    @pl.when(t == 0)
    def _():
        kmt_ref[...] = jnp.zeros_like(kmt_ref)

    lane_g = lax.broadcasted_iota(jnp.int32, (1, GROUP), 1) // DH_MOBA
    for r in range(blk_per_tile):
        km = jnp.mean(mk[r * MOBA_BLOCK:(r + 1) * MOBA_BLOCK], axis=0, keepdims=True)
        for h in range(H_MOBA):
            kmt_ref[pl.ds(h * SEL_STRIDE + blk0 + r, 1), :] = jnp.where(lane_g == h, km, 0.0)

    gate = _dot_nt(mq * (DH_MOBA ** -0.5), kmt_ref[...], precision=lax.Precision.HIGHEST)
    lane = lax.broadcasted_iota(jnp.int32, (tm, LANES), 1)
    row = lax.broadcasted_iota(jnp.int32, (tm, LANES), 0)
    n_in = lane % SEL_STRIDE
    own = blk0 + row // MOBA_BLOCK
    past = n_in < own
    g = jnp.where(past, gate, NEG_INF)
    cnt = jnp.zeros((tm, LANES), jnp.int32)
    for s in range(1, n_blk_seq):
        lo = pltpu.roll(g, s, 1)
        cnt += jnp.where((n_in >= s) & (lo >= g), 1, 0)
        hi = pltpu.roll(g, LANES - s, 1)
        cnt += jnp.where((n_in + s < SEL_STRIDE) & (hi > g), 1, 0)
    visible = (past & (cnt < MOBA_TOPK)) | (n_in == own)
    sel = jnp.where(visible, 0.0, NEG_INF).astype(BF16)
    er = lax.broadcasted_iota(jnp.int32, (LANES, GROUP), 0)
    ec = lax.broadcasted_iota(jnp.int32, (LANES, GROUP), 1)
    spread = jnp.where((ec // HALF == er // SEL_STRIDE) & (ec % HALF == AUG_SEL + er % SEL_STRIDE),
                       1.0, 0.0).astype(BF16)
    aug_ref[...] = (_dot(sel, spread) + atab_ref[...]).astype(BF16)


def _project_prompt(x2d, g, w_bf, aug_tab, *, tm, n_blk_seq):
    n, d = x2d.shape
    assert tm % MOBA_BLOCK == 0 and n_blk_seq % (tm // MOBA_BLOCK) == 0 and n_blk_seq <= SEL_STRIDE
    seq = n_blk_seq * MOBA_BLOCK
    tiles_per_seq = seq // tm
    row_spec = pl.BlockSpec((tm, GROUP), lambda t: (t, 0))
    t_spec = pl.BlockSpec((None, GROUP, tm), lambda t: (t // tiles_per_seq, 0, t % tiles_per_seq))
    f32_out = jax.ShapeDtypeStruct((n, GROUP), F32)
    bf_out = jax.ShapeDtypeStruct((n, GROUP), BF16)
    t_out = jax.ShapeDtypeStruct((n // seq, GROUP, seq), F32)
    return pl.pallas_call(
        functools.partial(_proj_prompt_kernel, tm=tm, n_blk_seq=n_blk_seq),
        grid=(n // tm,),
        in_specs=[pl.BlockSpec((tm, d), lambda t: (t, 0)),
                  pl.BlockSpec((1, d), lambda t: (0, 0)),
                  pl.BlockSpec((d, 6 * GROUP), lambda t: (0, 0)),
                  pl.BlockSpec((1, GROUP), lambda t: (0, 0))],
        out_specs=[row_spec, t_spec, row_spec, pl.BlockSpec((tm * H_DIFF, DV_DIFF), lambda t: (t, 0)),
                   row_spec, row_spec, t_spec, row_spec, t_spec, row_spec, row_spec],
        out_shape=[bf_out, t_out, bf_out, jax.ShapeDtypeStruct((n * H_DIFF, DV_DIFF), F32), bf_out, bf_out,
                   t_out, bf_out, t_out, bf_out, bf_out],
        scratch_shapes=[pltpu.VMEM((LANES, GROUP), F32)],
        compiler_params=_params(("arbitrary",)),
    )(x2d, g.reshape(1, d), w_bf, aug_tab)


def _proj_sample_kernel(x_ref, g_ref, w_ref, dq_ref, dk_ref, dv_ref, mq_ref, mk_ref, mv_ref):
    xn = _rms(x_ref[...], g_ref[...])
    scales = (DH_DIFF ** -0.5, 1.0, 1.0, DH_MOBA ** -0.5, 1.0, 1.0)
    for c, (ref, sc) in enumerate(zip((dq_ref, dk_ref, dv_ref, mq_ref, mk_ref, mv_ref), scales)):
        ref[...] = jnp.dot(xn, w_ref[:, c * GROUP:(c + 1) * GROUP], preferred_element_type=F32,
                           precision=lax.Precision.HIGHEST) * sc


def _project_sample(x2d, g, w_f32):
    n, d = x2d.shape
    out = jax.ShapeDtypeStruct((n, GROUP), F32)
    return pl.pallas_call(
        _proj_sample_kernel,
        grid=(1,),
        in_specs=[pl.BlockSpec((n, d), lambda t: (0, 0)),
                  pl.BlockSpec((1, d), lambda t: (0, 0)),
                  pl.BlockSpec((d, 6 * GROUP), lambda t: (0, 0))],
        out_specs=[pl.BlockSpec((n, GROUP), lambda t: (0, 0))] * 6,
        out_shape=[out] * 6,
        compiler_params=_params(("arbitrary",)),
    )(x2d, g.reshape(1, d), w_f32)


def _pair_attn_kernel(q_ref, aug_ref, k_ref, ktab_ref, v_ref, *rest, tq, moba, lam_init):
    if moba:
        o_ref, m_scr, acc_scr, s0_scr, s1_scr = rest
    else:
        lamp_ref, g_ref, o_ref, m_scr, acc_scr, s0_scr, s1_scr = rest
    i = pl.program_id(2)

    q = q_ref[...]
    if moba:
        aug = aug_ref[...]
    else:
        aug = jnp.broadcast_to(aug_ref[...], (tq, LANES)).astype(BF16)
    lane = lax.broadcasted_iota(jnp.int32, (tq, LANES), 1)
    lo = lane < HALF
    zero = jnp.zeros_like(q)
    q2 = jnp.concatenate(
        [jnp.concatenate([jnp.where(lo, q, zero), jnp.where(lo, aug, zero)], axis=1),
         jnp.concatenate([jnp.where(lo, zero, q), jnp.where(lo, zero, aug)], axis=1)], axis=0)

    m_scr[...] = jnp.full_like(m_scr, NEG_INF)
    acc_scr[...] = jnp.zeros_like(acc_scr)
    ones = jnp.ones((tq, LANES), BF16)
    n_rep = tq // LANES

    def scores(jb):
        start = pl.multiple_of(jb * tq, tq)
        kj = jnp.concatenate([k_ref[pl.ds(start, tq), :], ktab_ref[pl.ds(start, tq), :]], axis=1)
        return _dot_nt(q2, kj)

    def absorb(s, jb):
        start = pl.multiple_of(jb * tq, tq)
        vj = jnp.concatenate([v_ref[pl.ds(start, tq), :], ones], axis=1)
        m_old = m_scr[...]
        m_new = jnp.maximum(m_old, jnp.max(s, axis=-1, keepdims=True))
        alpha = jnp.exp2(m_old - m_new)
        pe = jnp.exp2(s - jnp.concatenate([m_new] * n_rep, axis=1))
        acc_scr[...] = jnp.concatenate([alpha, alpha], axis=1) * acc_scr[...] + _dot(pe.astype(BF16), vj)
        m_scr[...] = m_new

    r = lax.broadcasted_iota(jnp.int32, (2 * tq, tq), 0)
    c = lax.broadcasted_iota(jnp.int32, (2 * tq, tq), 1)
    s0_scr[...] = jnp.where(r % tq >= c, scores(i), NEG_INF)
    last = jnp.maximum(i - 1, 0)

    def body(u, carry):
        s1_scr[...] = scores(jnp.minimum(2 * u, last))
        absorb(s0_scr[...], jnp.where(u == 0, i, 2 * u - 1))

        @pl.when(2 * u + 1 <= i)
        def _():
            s0_scr[...] = scores(jnp.minimum(2 * u + 1, last))
            absorb(s1_scr[...], 2 * u)

        return carry

    lax.fori_loop(0, i // 2 + 1, body, 0)

    acc = acc_scr[...]
    o2 = acc[:, :LANES] / acc[:, LANES:]
    o0, o1 = o2[:tq], o2[tq:]
    if moba:
        o_ref[...] = jnp.where(lo, o0, o1).astype(o_ref.dtype)
    else:
        o = o0 - _lambda(lamp_ref[...], lam_init) * o1
        o_ref[...] = (_rms(o, g_ref[...]) * (1.0 - lam_init)).astype(o_ref.dtype)


def _pair_attention(q_bf, aug, k_bf, ktab, v_bf, *, nb, seq, tq, moba, lamp=None, g_subln=None,
                    lam_init=0.0):
    n_pair = GROUP // LANES
    nq = seq // tq
    assert tq % MOBA_BLOCK == 0 and seq % tq == 0
    q_spec = pl.BlockSpec((tq, LANES), lambda b, p, i: (b * nq + i, p))
    kv_spec = pl.BlockSpec((seq, LANES), lambda b, p, i: (b, p))
    if moba:
        aug_spec = q_spec
    else:
        aug_spec = pl.BlockSpec((None, 1, LANES), lambda b, p, i: (p, 0, 0))
    in_specs = [q_spec, aug_spec, kv_spec, pl.BlockSpec((seq, LANES), lambda b, p, i: (0, 0)), kv_spec]
    args = [q_bf, aug, k_bf, ktab, v_bf]
    if not moba:
        in_specs += [pl.BlockSpec((4, DH_DIFF), lambda b, p, i: (0, 0)),
                     pl.BlockSpec((1, DV_DIFF), lambda b, p, i: (0, 0))]
        args += [lamp, g_subln.reshape(1, DV_DIFF)]
    return pl.pallas_call(
        functools.partial(_pair_attn_kernel, tq=tq, moba=moba, lam_init=lam_init),
        grid=(nb, n_pair, nq),
        in_specs=in_specs,
        out_specs=q_spec,
        out_shape=jax.ShapeDtypeStruct((nb * seq, GROUP), BF16),
        scratch_shapes=[pltpu.VMEM((2 * tq, LANES), F32), pltpu.VMEM((2 * tq, 2 * LANES), F32),
                        pltpu.VMEM((2 * tq, tq), F32), pltpu.VMEM((2 * tq, tq), F32)],
        compiler_params=_params(("arbitrary", "arbitrary", "arbitrary")),
    )(*args)


def _norm_mm_kernel(x_ref, g_ref, w_ref, o_ref, *, scale, relu2):
    y = _dot(_rms(x_ref[...], g_ref[...]).astype(BF16), w_ref[...])
    if relu2:
        y = jnp.square(jnp.maximum(y, 0.0))
    if scale != 1.0:
        y = y * scale
    o_ref[...] = y.astype(o_ref.dtype)


def _norm_matmul(x2d, g, w_bf, *, tm, out_dtype, scale=1.0, relu2=False):
    n, d = x2d.shape
    e = w_bf.shape[1]
    return pl.pallas_call(
        functools.partial(_norm_mm_kernel, scale=scale, relu2=relu2),
        grid=(n // tm,),
        in_specs=[pl.BlockSpec((tm, d), lambda t: (t, 0)),
                  pl.BlockSpec((1, d), lambda t: (0, 0)),
                  pl.BlockSpec((d, e), lambda t: (0, 0))],
        out_specs=pl.BlockSpec((tm, e), lambda t: (t, 0)),
        out_shape=jax.ShapeDtypeStruct((n, e), out_dtype),
        compiler_params=_params(("arbitrary",)),
    )(x2d, g.reshape(1, d), w_bf)


def _mm_norm_res_kernel(*refs, n_in):
    a_refs = refs[:n_in]
    w_refs = refs[n_in:2 * n_in]
    x_ref, g_ref, o_ref = refs[2 * n_in:]
    y = _dot(a_refs[0][...].astype(BF16), w_refs[0][...])
    for a_ref, w_ref in zip(a_refs[1:], w_refs[1:]):
        y = y + _dot(a_ref[...].astype(BF16), w_ref[...])
    o_ref[...] = x_ref[...] + _rms(y, g_ref[...])


def _matmul_norm_residual(a_list, w_list, x2d, g, *, tm):
    n, d = x2d.shape
    n_in = len(a_list)
    in_specs = [pl.BlockSpec((tm, a.shape[1]), lambda t: (t, 0)) for a in a_list]
    in_specs += [pl.BlockSpec(w.shape, lambda t: (0, 0)) for w in w_list]
    in_specs += [pl.BlockSpec((tm, d), lambda t: (t, 0)), pl.BlockSpec((1, d), lambda t: (0, 0))]
    return pl.pallas_call(
        functools.partial(_mm_norm_res_kernel, n_in=n_in),
        grid=(n // tm,),
        in_specs=in_specs,
        out_specs=pl.BlockSpec((tm, d), lambda t: (t, 0)),
        out_shape=jax.ShapeDtypeStruct((n, d), F32),
        compiler_params=_params(("arbitrary",)),
    )(*a_list, *w_list, x2d, g.reshape(1, d))


def _mem_attn_kernel(q_ref, k_ref, v_ref, o_ref, *, interleaved):
    q = q_ref[...]
    outs = []
    for h in range(H_MEM):
        cols = slice(h * DH_MEM, (h + 1) * DH_MEM)
        if interleaved:
            kh = k_ref[pl.ds(h, N_MEM, stride=H_MEM), :]
            vh = v_ref[pl.ds(h, N_MEM, stride=H_MEM), :]
        else:
            kh, vh = k_ref[:, cols], v_ref[:, cols]
        s = _dot_nt(q[:, cols], kh.astype(BF16))
        pe = jnp.exp(s - jnp.max(s, axis=-1, keepdims=True))
        o = _dot(pe.astype(BF16), vh.astype(BF16))
        outs.append(o / jnp.sum(pe, axis=-1, keepdims=True))
    o_ref[...] = jnp.concatenate(outs, axis=1).astype(o_ref.dtype)


def _mem_attention(q_bf, mem_k, mem_v, *, nb, rows, tm):
    d = H_MEM * DH_MEM
    nt = rows // tm
    interleaved = mem_k.shape[1] == N_MEM * H_MEM
    mem_block = (None,) + mem_k.shape[1:]
    return pl.pallas_call(
        functools.partial(_mem_attn_kernel, interleaved=interleaved),
        grid=(nb, nt),
        in_specs=[pl.BlockSpec((tm, d), lambda b, t: (b * nt + t, 0)),
                  pl.BlockSpec(mem_block, lambda b, t: (b, 0, 0)),
                  pl.BlockSpec(mem_block, lambda b, t: (b, 0, 0))],
        out_specs=pl.BlockSpec((tm, d), lambda b, t: (b * nt + t, 0)),
        out_shape=jax.ShapeDtypeStruct((nb * rows, d), BF16),
        compiler_params=_params(("arbitrary", "arbitrary")),
    )(q_bf, mem_k, mem_v)


def _mlp_kernel(x_ref, g_pre_ref, w_up_ref, w_down_ref, g_post_ref, o_ref):
    x = x_ref[...]
    h = _dot(_rms(x, g_pre_ref[...]).astype(BF16), w_up_ref[...])
    h = jnp.square(jnp.maximum(h, 0.0)).astype(BF16)
    o_ref[...] = x + _rms(_dot(h, w_down_ref[...]), g_post_ref[...])


def _mlp(x2d, g_pre, w_up, w_down, g_post, *, tm):
    n, d = x2d.shape
    f = w_up.shape[1]
    const = lambda t: (0, 0)
    resident = pl.Buffered(1)
    return pl.pallas_call(
        _mlp_kernel,
        grid=(n // tm,),
        in_specs=[pl.BlockSpec((tm, d), lambda t: (t, 0)),
                  pl.BlockSpec((1, d), const),
                  pl.BlockSpec((d, f), const, pipeline_mode=resident),
                  pl.BlockSpec((f, d), const, pipeline_mode=resident),
                  pl.BlockSpec((1, d), const)],
        out_specs=pl.BlockSpec((tm, d), lambda t: (t, 0)),
        out_shape=jax.ShapeDtypeStruct((n, d), F32),
        compiler_params=_params(("arbitrary",)),
    )(x2d, g_pre.reshape(1, d), w_up, w_down, g_post.reshape(1, d))


def _finish_layer(x2d, o_diff, o_moba, mem_k, mem_v, wts, *, nb, rows, tm, rows_pad=1):
    n = nb * rows
    tm_thin = 2 * tm if n % (2 * tm) == 0 else tm
    x1 = _matmul_norm_residual([o_diff, o_moba], [wts['w_out_a'], wts['w_out_b']], x2d,
                               wts['g_post_mix'], tm=tm_thin)
    q = _norm_matmul(x1, wts['g_pre_x'], wts['w_xq'], tm=tm_thin, out_dtype=BF16, scale=DH_MEM ** -0.5)
    if rows_pad > 1:
        qp = jnp.broadcast_to(q[:, None, :], (nb * rows, rows_pad, q.shape[1])).reshape(-1, q.shape[1])
        c = _mem_attention(qp, mem_k, mem_v, nb=nb, rows=rows * rows_pad, tm=rows * rows_pad)
        c = c.reshape(nb * rows, rows_pad, -1)[:, 0, :]
    else:
        c = _mem_attention(q, mem_k, mem_v, nb=nb, rows=rows, tm=min(tm, rows))
    x2 = _matmul_norm_residual([c], [wts['w_xo']], x1, wts['g_post_x'], tm=tm_thin)
    return _mlp(x2, wts['g_pre_ff'], wts['w_up'], wts['w_down'], wts['g_post_ff'], tm=tm)


def _query_columns(q_row):
    r = lax.broadcasted_iota(jnp.int32, (GROUP, GROUP), 0)
    c = lax.broadcasted_iota(jnp.int32, (GROUP, GROUP), 1)
    col = jnp.sum(jnp.where(r == c, jnp.broadcast_to(q_row, (GROUP, GROUP)), 0.0), axis=-1, keepdims=True)
    return jnp.broadcast_to(col, (GROUP, LANES))


def _page_scores(kt, qb, n_grp):
    prod = kt * qb
    w = GROUP // n_grp
    return jnp.concatenate([jnp.sum(prod[g * w:(g + 1) * w], axis=0, keepdims=True) for g in range(n_grp)],
                           axis=0)


def _self_scores(q_row, k_row, n_grp):
    row = lax.broadcasted_iota(jnp.int32, (n_grp, GROUP), 0)
    lane = lax.broadcasted_iota(jnp.int32, (n_grp, GROUP), 1)
    prod = jnp.broadcast_to(q_row * k_row, (n_grp, GROUP))
    return jnp.sum(jnp.where(lane // (GROUP // n_grp) == row, prod, 0.0), axis=-1, keepdims=True)


def _diff_decode_kernel(pt_ref, q_ref, ks_ref, vs_ref, lamp_ref, g_ref, *rest, n_pg, past, lam_init):
    k_refs = rest[:n_pg]
    v_refs = rest[n_pg:2 * n_pg]
    o_ref, qb_scr, m_scr, l_scr, acc_scr = rest[2 * n_pg:]
    step = pl.program_id(1)
    n_rows = 2 * H_DIFF
    page = k_refs[0].shape[1]
    slopes = _alibi_slopes(H_DIFF)
    row1 = lax.broadcasted_iota(jnp.int32, (n_rows, 1), 0)
    slope_col = jnp.zeros((n_rows, 1), F32)
    for h in range(H_DIFF):
        slope_col = jnp.where(row1 // 2 == h, float(slopes[h]), slope_col)

    @pl.when(step == 0)
    def _():
        qb_scr[...] = _query_columns(q_ref[...])
        m_scr[...] = jnp.broadcast_to(_self_scores(q_ref[...], ks_ref[...], n_rows), (n_rows, LANES))
        l_scr[...] = jnp.ones_like(l_scr)
        for h in range(H_DIFF):
            acc_scr[h] = jnp.broadcast_to(vs_ref[:, h * DV_DIFF:(h + 1) * DV_DIFF], (n_rows, DV_DIFF))

    qb = qb_scr[...]
    s = jnp.concatenate([_page_scores(k_refs[pg][...], qb, n_rows) for pg in range(n_pg)], axis=1)
    kpos = step * (n_pg * page) + lax.broadcasted_iota(jnp.int32, (1, n_pg * page), 1)
    s = s - slope_col * (past - kpos).astype(F32)
    m_old = m_scr[...]
    m_new = jnp.maximum(m_old, jnp.max(s, axis=-1, keepdims=True))
    alpha = jnp.exp(m_old - m_new)
    pe = jnp.exp(s - jnp.concatenate([m_new] * n_pg, axis=1))
    l_scr[...] = alpha * l_scr[...] + jnp.sum(pe, axis=-1, keepdims=True)
    pe = pe.astype(BF16)
    for h in range(H_DIFF):
        pv = None
        for pg in range(n_pg):
            vh = v_refs[pg][pl.ds(h, page, stride=H_DIFF), :].astype(BF16)
            d = _dot(pe[:, pg * page:(pg + 1) * page], vh)
            pv = d if pv is None else pv + d
        acc_scr[h] = alpha * acc_scr[h] + pv
    m_scr[...] = m_new

    @pl.when(step == pl.num_programs(1) - 1)
    def _():
        lam = _lambda(lamp_ref[...], lam_init)
        row = lax.broadcasted_iota(jnp.int32, (n_rows, DV_DIFF), 0)
        inv_l = 1.0 / l_scr[...]
        parts = []
        for h in range(H_DIFF):
            coef = jnp.where(row == 2 * h, 1.0, jnp.where(row == 2 * h + 1, -lam, 0.0))
            o = jnp.sum(acc_scr[h] * inv_l * coef, axis=0, keepdims=True)
            parts.append(_rms(o, g_ref[...]))
        o_ref[...] = jnp.concatenate(parts, axis=1) * (1.0 - lam_init)


def _diff_decode(page_table, q, k_self, v_self, lamp, g_subln, cache_kt, cache_v, *,
                 layer_off, n_pg, lam_init):
    nb, n_pages = page_table.shape
    page = cache_kt.shape[2]
    past = n_pages * page
    assert n_pages % n_pg == 0 and page == LANES

    def page_spec(pg, rows, cols):
        return pl.BlockSpec((None, rows, cols),
                            lambda b, s, pt: (layer_off + pt[b * n_pages + s * n_pg + pg], 0, 0))

    vec_spec = pl.BlockSpec((None, 1, GROUP), lambda b, s, pt: (b, 0, 0))
    n_rows = 2 * H_DIFF
    grid_spec = pltpu.PrefetchScalarGridSpec(
        num_scalar_prefetch=1,
        grid=(nb, n_pages // n_pg),
        in_specs=[vec_spec, vec_spec, vec_spec,
                  pl.BlockSpec((4, DH_DIFF), lambda b, s, pt: (0, 0)),
                  pl.BlockSpec((1, DV_DIFF), lambda b, s, pt: (0, 0))]
                 + [page_spec(pg, GROUP, page) for pg in range(n_pg)]
                 + [page_spec(pg, page * H_DIFF, DV_DIFF) for pg in range(n_pg)],
        out_specs=vec_spec,
        scratch_shapes=[pltpu.VMEM((GROUP, LANES), F32), pltpu.VMEM((n_rows, LANES), F32),
                        pltpu.VMEM((n_rows, LANES), F32), pltpu.VMEM((H_DIFF, n_rows, DV_DIFF), F32)],
    )
    out = pl.pallas_call(
        functools.partial(_diff_decode_kernel, n_pg=n_pg, past=past, lam_init=lam_init),
        grid_spec=grid_spec,
        out_shape=jax.ShapeDtypeStruct((nb, 1, GROUP), F32),
        compiler_params=_params(("arbitrary", "arbitrary")),
    )(page_table.reshape(-1), q.reshape(nb, 1, GROUP), k_self.reshape(nb, 1, GROUP),
      v_self.reshape(nb, 1, GROUP), lamp, g_subln.reshape(1, DV_DIFF),
      *([cache_kt] * n_pg), *([cache_v] * n_pg))
    return out.reshape(nb, GROUP)


def _moba_scores_kernel(pt_ref, q_ref, ks_ref, *rest, n_pg, n_pages, past):
    k_refs = rest[:n_pg]
    p_ref, sel_ref, pself_ref, qb_scr, s_scr = rest[n_pg:]
    step = pl.program_id(1)
    page = k_refs[0].shape[1]

    @pl.when(step == 0)
    def _():
        qb_scr[...] = _query_columns(q_ref[...])

    qb = qb_scr[...]
    for pg in range(n_pg):
        s_scr[step * n_pg + pg] = _page_scores(k_refs[pg][...], qb, H_MOBA)

    @pl.when(step == pl.num_programs(1) - 1)
    def _():
        slopes = _alibi_slopes(H_MOBA)
        s_self = _self_scores(q_ref[...], ks_ref[...], H_MOBA)
        pg_i = lax.broadcasted_iota(jnp.int32, (n_pages, page), 0)
        pg_f = pg_i.astype(F32)
        blk_f = jnp.floor(pg_f * 0.5)
        dist = (past - (pg_i * page + lax.broadcasted_iota(jnp.int32, (n_pages, page), 1))).astype(F32)
        row = lax.broadcasted_iota(jnp.int32, (H_MOBA, LANES), 0)
        lane = lax.broadcasted_iota(jnp.int32, (H_MOBA, LANES), 1)
        sel_out = jnp.zeros((H_MOBA, LANES), F32)
        pself_out = jnp.zeros((H_MOBA, LANES), F32)
        for h in range(H_MOBA):
            sh = s_scr[:, h, :]
            rs = jnp.broadcast_to(jnp.sum(sh, axis=-1, keepdims=True), (n_pages, page))
            other = jnp.where(pg_i % 2 == 0, pltpu.roll(rs, n_pages - 1, 0), pltpu.roll(rs, 1, 0))
            g = (rs + other) * (1.0 / MOBA_BLOCK)
            chosen = jnp.zeros((n_pages, page), jnp.bool_)
            for r in range(MOBA_TOPK):
                mx = jnp.max(g, axis=0, keepdims=True)
                first = jnp.min(jnp.where(g == mx, pg_f, float(n_pages)), axis=0, keepdims=True)
                idx = jnp.floor(first * 0.5)
                pick = blk_f == idx
                chosen = chosen | pick
                g = jnp.where(pick, REMOVED, g)
                sel_out = jnp.where((row == h) & (lane == r), idx, sel_out)
            logit = jnp.where(chosen, sh - float(slopes[h]) * dist, NEG_INF)
            ss = s_self[h:h + 1]
            m = jnp.maximum(jnp.max(jnp.max(logit, axis=-1, keepdims=True), axis=0, keepdims=True), ss)
            pe = jnp.exp(logit - m)
            ps = jnp.exp(ss - m)
            inv = 1.0 / (jnp.sum(jnp.sum(pe, axis=-1, keepdims=True), axis=0, keepdims=True) + ps)
            p_ref[h] = pe * inv
            pself_out = jnp.where(row == h, ps * inv, pself_out)
        sel_ref[...] = sel_out.astype(jnp.int32)
        pself_ref[...] = pself_out


def _moba_scores(page_table, q, k_self, cache_kt, *, layer_off, n_pg):
    nb, n_pages = page_table.shape
    page = cache_kt.shape[2]
    past = n_pages * page
    n_blk = past // MOBA_BLOCK
    pg_per_blk = MOBA_BLOCK // page
    assert n_blk >= MOBA_TOPK and n_pages % n_pg == 0 and pg_per_blk == 2 and page == LANES

    def page_spec(pg):
        return pl.BlockSpec((None, GROUP, page),
                            lambda b, s, pt: (layer_off + pt[b * n_pages + s * n_pg + pg], 0, 0))

    vec_spec = pl.BlockSpec((None, 1, GROUP), lambda b, s, pt: (b, 0, 0))
    small_spec = pl.BlockSpec((None, H_MOBA, LANES), lambda b, s, pt: (b, 0, 0))
    grid_spec = pltpu.PrefetchScalarGridSpec(
        num_scalar_prefetch=1,
        grid=(nb, n_pages // n_pg),
        in_specs=[vec_spec, vec_spec] + [page_spec(pg) for pg in range(n_pg)],
        out_specs=[pl.BlockSpec((None, H_MOBA, n_pages, page), lambda b, s, pt: (b, 0, 0, 0)),
                   small_spec, small_spec],
        scratch_shapes=[pltpu.VMEM((GROUP, LANES), F32), pltpu.VMEM((n_pages, H_MOBA, page), F32)],
    )
    return pl.pallas_call(
        functools.partial(_moba_scores_kernel, n_pg=n_pg, n_pages=n_pages, past=past),
        grid_spec=grid_spec,
        out_shape=[jax.ShapeDtypeStruct((nb, H_MOBA, n_pages, page), F32),
                   jax.ShapeDtypeStruct((nb, H_MOBA, LANES), jnp.int32),
                   jax.ShapeDtypeStruct((nb, H_MOBA, LANES), F32)],
        compiler_params=_params(("arbitrary", "arbitrary")),
    )(page_table.reshape(-1), q.reshape(nb, 1, GROUP), k_self.reshape(nb, 1, GROUP),
      *([cache_kt] * n_pg))


def _moba_values_kernel(pt_ref, sel_ref, p_ref, pself_ref, vs_ref, *rest, n_chunk):
    v_refs = rest[:n_chunk]
    o_ref = rest[n_chunk]
    b = pl.program_id(0)
    h0 = pl.program_id(1) * (2 * VALUE_PAIRS)
    page = v_refs[0].shape[1]
    pg_per_blk = MOBA_BLOCK // page
    lane = lax.broadcasted_iota(jnp.int32, (SUBLANES, LANES), 1)
    outs = []
    for pp_i in range(VALUE_PAIRS):
        pair = jnp.zeros((SUBLANES, LANES), F32)
        for hf in range(2):
            hl = 2 * pp_i + hf
            h = h0 + hl
            acc = pself_ref[pl.ds(h, 1), :] * vs_ref[:, pp_i * LANES:(pp_i + 1) * LANES]
            acc = jnp.broadcast_to(acc, (SUBLANES, LANES))
            for r in range(MOBA_TOPK):
                blk = sel_ref[(b * H_MOBA + h) * MOBA_TOPK + r]
                for pg in range(pg_per_blk):
                    pp = jnp.broadcast_to(p_ref[hl, pl.ds(blk * pg_per_blk + pg, 1), :], (SUBLANES, page))
                    vt = v_refs[(hl * MOBA_TOPK + r) * pg_per_blk + pg][...]
                    acc = acc + _dot_nt(pp.astype(BF16), vt.astype(BF16))
            pair = jnp.where(lane // HALF == hf, acc, pair)
        outs.append(pair[0:1])
    o_ref[...] = jnp.concatenate(outs, axis=1)


def _moba_values(page_table, sel, p, pself, v_self, cache_vt, *, layer_off):
    nb, n_pages = page_table.shape
    page = cache_vt.shape[2]
    pg_per_blk = MOBA_BLOCK // page
    heads_per_step = 2 * VALUE_PAIRS
    n_step = H_MOBA // heads_per_step
    n_chunk = heads_per_step * MOBA_TOPK * pg_per_blk

    def chunk_spec(hl, r, pg):
        def index_map(b, st, pt, sl):
            blk = sl[(b * H_MOBA + st * heads_per_step + hl) * MOBA_TOPK + r]
            return (layer_off + pt[b * n_pages + blk * pg_per_blk + pg], st * VALUE_PAIRS + hl // 2, 0)
        return pl.BlockSpec((None, LANES, page), index_map)

    pair_vec = pl.BlockSpec((None, 1, VALUE_PAIRS * LANES), lambda b, st, pt, sl: (b, 0, st))
    grid_spec = pltpu.PrefetchScalarGridSpec(
        num_scalar_prefetch=2,
        grid=(nb, n_step),
        in_specs=[pl.BlockSpec((None, heads_per_step, n_pages, page), lambda b, st, pt, sl: (b, st, 0, 0)),
                  pl.BlockSpec((None, H_MOBA, LANES), lambda b, st, pt, sl: (b, 0, 0)),
                  pair_vec]
                 + [chunk_spec(hl, r, pg) for hl in range(heads_per_step) for r in range(MOBA_TOPK)
                    for pg in range(pg_per_blk)],
        out_specs=pair_vec,
    )
    out = pl.pallas_call(
        functools.partial(_moba_values_kernel, n_chunk=n_chunk),
        grid_spec=grid_spec,
        out_shape=jax.ShapeDtypeStruct((nb, 1, GROUP), F32),
        compiler_params=_params(("arbitrary", "arbitrary")),
    )(page_table.reshape(-1), sel.reshape(-1), p, pself, v_self.reshape(nb, 1, GROUP),
      *([cache_vt] * n_chunk))
    return out.reshape(nb, GROUP)


PROMPT_TILE = 512
ATTN_TILE = 512
DIFF_DECODE_PAGES = 32
MOBA_DECODE_PAGES = 64
VALUE_PAIRS = 2


def _feature_major(cache, n_lead):
    nd = cache.ndim
    perm = (0, 1) + tuple(range(3, nd)) + (2,)
    t = jnp.transpose(cache, perm)
    return t.reshape(n_lead, -1, cache.shape[2])


def kernel(x_prompt, x_sample, cache_diff_k, cache_diff_v, cache_moba_k, cache_moba_v, cache_mem_k, cache_mem_v, page_table, mem_prompt, g_pre_mix, w_in, lambda_q1, lambda_k1, lambda_q2, lambda_k2, g_subln, w_out, g_post_mix, g_mem, w_mem_k, w_mem_v, g_pre_x, w_xq, w_xo, g_post_x, g_pre_ff, w_up, w_down, g_post_ff):
    nb_p, seq, d = x_prompt.shape
    nb_s, s_len, _ = x_sample.shape
    assert s_len == 1, "the sample kernels handle one new token per sequence"
    depth = w_in.shape[0]
    n_pool, page = cache_diff_k.shape[1], cache_diff_k.shape[2]
    assert MOBA_BLOCK % page == 0 and (page_table.shape[1] * page) % MOBA_BLOCK == 0
    n_blk_seq = seq // MOBA_BLOCK
    tm = min(PROMPT_TILE, seq)
    tq = min(ATTN_TILE, seq)

    xp = x_prompt.reshape(nb_p * seq, d)
    xs = x_sample.reshape(nb_s, d)
    ckt_d = _feature_major(cache_diff_k, depth * n_pool)
    cv_d = cache_diff_v.reshape(depth * n_pool, page * H_DIFF, DV_DIFF)
    ckt_m = _feature_major(cache_moba_k, depth * n_pool)
    cvt_m = _feature_major(cache_moba_v, depth * n_pool)
    mem_x = mem_prompt.reshape(nb_p * N_MEM, d)
    slopes_d = _alibi_slopes(H_DIFF)
    slopes_m = _alibi_slopes(H_MOBA)
    ktab = _key_aug_table(seq)
    aug_d = jnp.asarray(_query_aug_table(np.stack([slopes_d, slopes_d], axis=1)))
    aug_m = jnp.asarray(_query_aug_table(slopes_m.reshape(-1, 2)).reshape(1, GROUP))

    outs = [[] for _ in range(10)]
    for l in range(depth):
        lam_init = 0.8 - 0.6 * math.exp(-0.3 * l)
        lamp = jnp.stack([lambda_q1[l], lambda_k1[l], lambda_q2[l], lambda_k2[l]]).astype(F32)
        w_in_bf = w_in[l].astype(BF16)
        w_out_bf = w_out[l].astype(BF16)
        wts = {
            'w_out_a': w_out_bf[:GROUP], 'w_out_b': w_out_bf[GROUP:], 'g_post_mix': g_post_mix[l],
            'g_pre_x': g_pre_x[l], 'w_xq': w_xq[l].astype(BF16), 'w_xo': w_xo[l].astype(BF16),
            'g_post_x': g_post_x[l], 'g_pre_ff': g_pre_ff[l], 'w_up': w_up[l].astype(BF16),
            'w_down': w_down[l].astype(BF16), 'g_post_ff': g_post_ff[l],
        }

        (dq, dk_t, dk_bf, dv, dv_bf, mq, mk_t, mk_bf, mv_t, mv_bf, mq_aug) = _project_prompt(
            xp, g_pre_mix[l], w_in_bf, aug_m, tm=tm, n_blk_seq=n_blk_seq)
        o_d = _pair_attention(dq, aug_d, dk_bf, ktab, dv_bf, nb=nb_p, seq=seq, tq=tq, moba=False,
                              lamp=lamp, g_subln=g_subln[l], lam_init=lam_init)
        o_m = _pair_attention(mq, mq_aug, mk_bf, ktab, mv_bf, nb=nb_p, seq=seq, tq=tq, moba=True)
        mem_tm = min(PROMPT_TILE, nb_p * N_MEM)
        memk = _norm_matmul(mem_x, g_mem[l], w_mem_k[l].astype(BF16), tm=mem_tm, out_dtype=F32)
        memv = _norm_matmul(mem_x, g_mem[l], w_mem_v[l].astype(BF16), tm=mem_tm, out_dtype=F32)
        xp = _finish_layer(xp, o_d, o_m, memk.reshape(nb_p, N_MEM, -1), memv.reshape(nb_p, N_MEM, -1),
                           wts, nb=nb_p, rows=seq, tm=tm)
        outs[0].append(jnp.transpose(dk_t.reshape(nb_p, H_DIFF, 2, DH_DIFF, seq), (0, 4, 1, 2, 3)))
        outs[1].append(dv.reshape(nb_p, seq, H_DIFF, DV_DIFF))
        outs[2].append(jnp.transpose(mk_t.reshape(nb_p, H_MOBA, DH_MOBA, seq), (0, 3, 1, 2)))
        outs[3].append(jnp.transpose(mv_t.reshape(nb_p, H_MOBA, DH_MOBA, seq), (0, 3, 1, 2)))
        outs[4].append(memk.reshape(nb_p, N_MEM, H_MEM, DH_MEM))
        outs[5].append(memv.reshape(nb_p, N_MEM, H_MEM, DH_MEM))

        sq, sk, sv, tq_s, tk, tv = _project_sample(xs, g_pre_mix[l], w_in[l])
        o_ds = _diff_decode(page_table, sq, sk, sv, lamp, g_subln[l], ckt_d, cv_d,
                            layer_off=l * n_pool, n_pg=math.gcd(DIFF_DECODE_PAGES, page_table.shape[1]),
                            lam_init=lam_init)
        p_s, sel_s, pself_s = _moba_scores(page_table, tq_s, tk, ckt_m, layer_off=l * n_pool,
                                           n_pg=math.gcd(MOBA_DECODE_PAGES, page_table.shape[1]))
        o_ms = _moba_values(page_table, sel_s[:, :, :MOBA_TOPK], p_s, pself_s, tv, cvt_m,
                            layer_off=l * n_pool)
        xs = _finish_layer(xs, o_ds, o_ms, cache_mem_k[l].reshape(nb_s, N_MEM * H_MEM, DH_MEM),
                           cache_mem_v[l].reshape(nb_s, N_MEM * H_MEM, DH_MEM), wts, nb=nb_s, rows=1, tm=nb_s,
                           rows_pad=SUBLANES)
        outs[6].append(sk.reshape(nb_s, 1, H_DIFF, 2, DH_DIFF))
        outs[7].append(sv.reshape(nb_s, 1, H_DIFF, DV_DIFF))
        outs[8].append(tk.reshape(nb_s, 1, H_MOBA, DH_MOBA))
        outs[9].append(tv.reshape(nb_s, 1, H_MOBA, DH_MOBA))

    return (xp.reshape(nb_p, seq, d), xs.reshape(nb_s, 1, d), *[jnp.stack(o) for o in outs])
```
